```python
import jax, jax.numpy as jnp
from jax import lax
import numpy as np

D_MODEL = 1024
BATCH = 2
SEQ = 8192
DEPTH = 1
DEC_BATCH = 16
DEC_SEQ = 64
PAST_LEN = 4096

CHUNK = 64
GMLP_CHUNK = 128
GMLP_WIDTH = 1024
GMLP_GROUPS = 8
GMLP_GROUP_DIM = GMLP_WIDTH // GMLP_GROUPS
SSM_INNER = 2 * D_MODEL
SSM_HEAD_DIM = 64
SSM_HEADS = SSM_INNER // SSM_HEAD_DIM
SSM_GROUPS = 4
SSM_HEADS_PER_GROUP = SSM_HEADS // SSM_GROUPS
SSM_STATE = 128
CONV_WIDTH = 4
CONV_DIM = SSM_INNER + 2 * SSM_GROUPS * SSM_STATE
FFN_HIDDEN = 4 * D_MODEL
IN_DIM = 2 * GMLP_WIDTH + SSM_INNER + CONV_DIM + SSM_HEADS + 2 * D_MODEL
EPS = 1e-6

kernel_name = "streaming_gmlp_ssd_hybrid_step"


def rms_norm(x, w):
    xf = x.astype(jnp.float32)
    y = xf * lax.rsqrt(jnp.mean(xf * xf, axis=-1, keepdims=True) + EPS)
    return (y * w.astype(jnp.float32)).astype(x.dtype)


def layer_norm(x, w, b):
    xf = x.astype(jnp.float32)
    mu = jnp.mean(xf, axis=-1, keepdims=True)
    xc = xf - mu
    var = jnp.mean(xc * xc, axis=-1, keepdims=True)
    return (xc * lax.rsqrt(var + EPS) * w.astype(jnp.float32) + b.astype(jnp.float32)).astype(x.dtype)


def gated_group_rms_norm(y, z, w):
    g = y.astype(jnp.float32) * jax.nn.silu(z.astype(jnp.float32))
    g = g.reshape(g.shape[:-1] + (SSM_GROUPS, SSM_INNER // SSM_GROUPS))
    g = g * lax.rsqrt(jnp.mean(g * g, axis=-1, keepdims=True) + EPS)
    return (g.reshape(y.shape) * w.astype(jnp.float32)).astype(y.dtype)


def causal_conv(xbc, hist, w, b):
    L = xbc.shape[1]
    xp = jnp.concatenate([hist.astype(xbc.dtype), xbc], axis=1)
    out = b
    for k in range(CONV_WIDTH):
        out = out + xp[:, k:k + L] * w[k]
    return jax.nn.silu(out), xp[:, -(CONV_WIDTH - 1):]


def gmlp_spatial(v, ws, bs):
    bsz, L, _ = v.shape
    n = min(L, GMLP_CHUNK)
    pos = jnp.arange(n)
    mask = (pos[None, :] // CHUNK) <= (pos[:, None] // CHUNK)
    w = jnp.where(mask[None], ws[:, :n, :n], 0.0).astype(v.dtype)
    vc = v.reshape(bsz, L // n, n, GMLP_GROUPS, GMLP_GROUP_DIM)
    out = jnp.einsum('gij,bcjgd->bcigd', w, vc) + bs[:, :n].T[None, None, :, :, None].astype(v.dtype)
    return out.reshape(bsz, L, GMLP_WIDTH)


def ssd(x, dt, a, bmat, cmat, h0, q):
    bsz, L = x.shape[:2]
    nc = L // q
    G, R, P, N = SSM_GROUPS, SSM_HEADS_PER_GROUP, SSM_HEAD_DIM, SSM_STATE
    x = x.reshape(bsz, nc, q, G, R, P)
    dt = dt.reshape(bsz, nc, q, G, R)
    bm = bmat.reshape(bsz, nc, q, G, N)
    cm = cmat.reshape(bsz, nc, q, G, N)
    acum = jnp.cumsum(dt * a.reshape(G, R), axis=2)
    idx = jnp.arange(q)
    causal = (idx[:, None] >= idx[None, :])[None, None, :, :, None, None]
    seg = acum[:, :, :, None] - acum[:, :, None, :]
    decay = jnp.exp(jnp.where(causal, seg, -jnp.inf))
    xdt = x * dt[..., None]
    cb = jnp.einsum('bcign,bcjgn->bcijg', cm, bm)
    y_diag = jnp.einsum('bcijgr,bcjgrp->bcigrp', cb[..., None] * decay, xdt)
    last = acum[:, :, -1]
    to_end = jnp.exp(last[:, :, None] - acum)
    chunk_states = jnp.einsum('bcjgn,bcjgrp->bcgrpn', bm, to_end[..., None] * xdt)

    def step(h, inp):
        dec, s = inp
        return h * jnp.exp(dec)[..., None, None] + s, h

    h_last, h_prev = lax.scan(step, h0.reshape(bsz, G, R, P, N),
                              (jnp.moveaxis(last, 1, 0), jnp.moveaxis(chunk_states, 1, 0)))
    h_prev = jnp.moveaxis(h_prev, 0, 1)
    y_off = jnp.einsum('bcign,bcgrpn->bcigrp', cm, h_prev) * jnp.exp(acum)[..., None]
    y = (y_diag + y_off).reshape(bsz, L, SSM_HEADS, P)
    return y, h_last.reshape(bsz, SSM_HEADS, P, N)


def hybrid_layer(x, conv_hist, h0, ssd_q, pre_mix_w, w_in, gmlp_ln_w, gmlp_ln_b, gmlp_ws, gmlp_bs,
                 conv_w, conv_b, dt_bias, a_log, d_skip, ssm_norm_w, w_branch_a, w_branch_b, w_out,
                 post_mix_w, pre_ffn_w, w_up, w_down, post_ffn_w):
    bsz, L, _ = x.shape
    h = rms_norm(x, pre_mix_w)
    proj = h @ w_in
    p1 = GMLP_WIDTH
    p2 = p1 + GMLP_WIDTH
    p3 = p2 + SSM_INNER
    p4 = p3 + CONV_DIM
    p5 = p4 + SSM_HEADS
    p6 = p5 + D_MODEL
    u, v, z, xbc, dt_raw, g_a, g_b = jnp.split(proj, [p1, p2, p3, p4, p5, p6], axis=-1)
    u = jax.nn.gelu(u)
    v = layer_norm(jax.nn.gelu(v), gmlp_ln_w, gmlp_ln_b)
    o_a = u * gmlp_spatial(v, gmlp_ws, gmlp_bs)
    xbc, conv_state = causal_conv(xbc, conv_hist, conv_w, conv_b)
    xs, bm, cm = jnp.split(xbc, [SSM_INNER, SSM_INNER + SSM_GROUPS * SSM_STATE], axis=-1)
    dt = jax.nn.softplus(dt_raw.astype(jnp.float32) + dt_bias.astype(jnp.float32))
    a = -jnp.exp(a_log.astype(jnp.float32))
    xs4 = xs.reshape(bsz, L, SSM_HEADS, SSM_HEAD_DIM).astype(jnp.float32)
    y, h_last = ssd(xs4, dt, a,
                    bm.reshape(bsz, L, SSM_GROUPS, SSM_STATE).astype(jnp.float32),
                    cm.reshape(bsz, L, SSM_GROUPS, SSM_STATE).astype(jnp.float32),
                    h0.astype(jnp.float32), ssd_q)
    y = (y + d_skip.astype(jnp.float32)[:, None] * xs4).reshape(bsz, L, SSM_INNER).astype(x.dtype)
    o_b = gated_group_rms_norm(y, z, ssm_norm_w)
    merged = jax.nn.sigmoid(g_a) * (o_a @ w_branch_a) + jax.nn.sigmoid(g_b) * (o_b @ w_branch_b)
    x = x + rms_norm(merged @ w_out, post_mix_w)
    f = jnp.square(jax.nn.relu(rms_norm(x, pre_ffn_w) @ w_up)) @ w_down
    x = x + rms_norm(f, post_ffn_w)
    return x, conv_state, h_last.astype(h0.dtype), v


def setup_inputs(seed: int = 0) -> dict:
    key = jax.random.key(seed)
    ks = jax.random.split(key, 32)
    nrm = lambda k, shape, s: jax.random.normal(k, shape, jnp.float32) * s
    dt0 = jnp.exp(jax.random.uniform(ks[10], (DEPTH, SSM_HEADS), jnp.float32)
                  * (np.log(0.1) - np.log(0.001)) + np.log(0.001))
    return {
        "x_prompt": nrm(ks[0], (BATCH, SEQ, D_MODEL), 1.0),
        "x_sample": nrm(ks[1], (DEC_BATCH, DEC_SEQ, D_MODEL), 1.0),
        "cache_conv": nrm(ks[2], (DEPTH, DEC_BATCH, CONV_WIDTH - 1, CONV_DIM), 1.0),
        "state_ssm": nrm(ks[3], (DEPTH, DEC_BATCH, SSM_HEADS, SSM_HEAD_DIM, SSM_STATE), 0.1),
        "pre_mix_w": 1.0 + nrm(ks[4], (DEPTH, D_MODEL), 0.02),
        "w_in": nrm(ks[5], (DEPTH, D_MODEL, IN_DIM), D_MODEL ** -0.5),
        "gmlp_ln_w": 1.0 + nrm(ks[6], (DEPTH, GMLP_WIDTH), 0.02),
        "gmlp_ln_b": nrm(ks[7], (DEPTH, GMLP_WIDTH), 0.02),
        "gmlp_ws": nrm(ks[8], (DEPTH, GMLP_GROUPS, GMLP_CHUNK, GMLP_CHUNK), GMLP_CHUNK ** -0.5),
        "gmlp_bs": 1.0 + nrm(ks[9], (DEPTH, GMLP_GROUPS, GMLP_CHUNK), 0.02),
        "conv_w": nrm(ks[11], (DEPTH, CONV_WIDTH, CONV_DIM), CONV_WIDTH ** -0.5),
        "conv_b": nrm(ks[12], (DEPTH, CONV_DIM), 0.01),
        "dt_bias": dt0 + jnp.log(-jnp.expm1(-dt0)),
        "a_log": jnp.log(jax.random.uniform(ks[13], (DEPTH, SSM_HEADS), jnp.float32, 1.0, 16.0)),
        "d_skip": 1.0 + nrm(ks[14], (DEPTH, SSM_HEADS), 0.1),
        "ssm_norm_w": 1.0 + nrm(ks[15], (DEPTH, SSM_INNER), 0.02),
        "w_branch_a": nrm(ks[16], (DEPTH, GMLP_WIDTH, D_MODEL), GMLP_WIDTH ** -0.5),
        "w_branch_b": nrm(ks[17], (DEPTH, SSM_INNER, D_MODEL), SSM_INNER ** -0.5),
        "w_out": nrm(ks[18], (DEPTH, D_MODEL, D_MODEL), D_MODEL ** -0.5),
        "post_mix_w": 1.0 + nrm(ks[19], (DEPTH, D_MODEL), 0.02),
        "pre_ffn_w": 1.0 + nrm(ks[20], (DEPTH, D_MODEL), 0.02),
        "w_up": nrm(ks[21], (DEPTH, D_MODEL, FFN_HIDDEN), D_MODEL ** -0.5),
        "w_down": nrm(ks[22], (DEPTH, FFN_HIDDEN, D_MODEL), FFN_HIDDEN ** -0.5),
        "post_ffn_w": 1.0 + nrm(ks[23], (DEPTH, D_MODEL), 0.02),
    }


def reference(x_prompt, x_sample, cache_conv, state_ssm, pre_mix_w, w_in, gmlp_ln_w, gmlp_ln_b,
              gmlp_ws, gmlp_bs, conv_w, conv_b, dt_bias, a_log, d_skip, ssm_norm_w, w_branch_a,
              w_branch_b, w_out, post_mix_w, pre_ffn_w, w_up, w_down, post_ffn_w):
    yp, ys = x_prompt, x_sample
    conv_hist0 = jnp.zeros((x_prompt.shape[0], CONV_WIDTH - 1, CONV_DIM), x_prompt.dtype)
    h00 = jnp.zeros((x_prompt.shape[0], SSM_HEADS, SSM_HEAD_DIM, SSM_STATE), state_ssm.dtype)
    conv_p, ssm_p, conv_s, ssm_s, v_s = [], [], [], [], []
    for l in range(DEPTH):
        lw = (pre_mix_w[l], w_in[l], gmlp_ln_w[l], gmlp_ln_b[l], gmlp_ws[l], gmlp_bs[l], conv_w[l],
              conv_b[l], dt_bias[l], a_log[l], d_skip[l], ssm_norm_w[l], w_branch_a[l], w_branch_b[l],
              w_out[l], post_mix_w[l], pre_ffn_w[l], w_up[l], w_down[l], post_ffn_w[l])
        yp, cp, sp, _ = hybrid_layer(yp, conv_hist0, h00, CHUNK, *lw)
        ys, cs, ss, vs = hybrid_layer(ys, cache_conv[l], state_ssm[l], x_sample.shape[1], *lw)
        conv_p.append(cp)
        ssm_p.append(sp)
        conv_s.append(cs)
        ssm_s.append(ss)
        v_s.append(vs)
    return (yp, ys, jnp.stack(conv_p), jnp.stack(ssm_p), jnp.stack(conv_s), jnp.stack(ssm_s), jnp.stack(v_s))
```

```python
import functools

import jax
import jax.numpy as jnp
from jax import lax
from jax.experimental import pallas as pl
from jax.experimental.pallas import tpu as pltpu

D_MODEL = 1024
CHUNK = 64
GMLP_CHUNK = 128
GMLP_WIDTH = 1024
GMLP_GROUPS = 8
GMLP_GROUP_DIM = GMLP_WIDTH // GMLP_GROUPS
SSM_INNER = 2 * D_MODEL
SSM_HEAD_DIM = 64
SSM_HEADS = SSM_INNER // SSM_HEAD_DIM
SSM_GROUPS = 4
SSM_HEADS_PER_GROUP = SSM_HEADS // SSM_GROUPS
SSM_STATE = 128
SSM_GROUP_WIDTH = SSM_INNER // SSM_GROUPS
BC_WIDTH = SSM_GROUPS * SSM_STATE
CONV_WIDTH = 4
CONV_DIM = SSM_INNER + 2 * BC_WIDTH
FFN_HIDDEN = 4 * D_MODEL
EPS = 1e-6

LANES = 128
SUBLANES = 8
TILE_M = 256
SEG_TILE_M = 128
FFN_CHUNK = 1024
VMEM_LIMIT = 56 * 1024 * 1024

BF16 = jnp.bfloat16
F32 = jnp.float32


def _dot(a, b):
    return jnp.dot(a, b, preferred_element_type=F32)


def _dot_nt(a, b):
    return lax.dot_general(a, b, (((1,), (1,)), ((), ())), preferred_element_type=F32)


def _rms_norm(x, w):
    return x * lax.rsqrt(jnp.mean(x * x, axis=-1, keepdims=True) + EPS) * w


def _sigmoid(x):
    return 1.0 / (1.0 + jnp.exp(-x))


def _silu(x):
    return x * _sigmoid(x)


def _gelu_tanh(x):
    c = 0.7978845608028654
    return 0.5 * x * (1.0 + jnp.tanh(c * (x + 0.044715 * (x * x * x))))


def _softplus(x):
    return jnp.maximum(x, 0.0) + jnp.log(1.0 + jnp.exp(-jnp.abs(x)))


def _split3(x):
    hi = x.astype(BF16)
    r1 = x - hi.astype(F32)
    mid = r1.astype(BF16)
    lo = (r1 - mid.astype(F32)).astype(BF16)
    return jnp.concatenate([hi, mid, lo], axis=1)


def _resident(shape):
    nd = len(shape)
    return pl.BlockSpec(shape, lambda i: (0,) * nd, pipeline_mode=pl.Buffered(1))


def _rows(tm, width):
    return pl.BlockSpec((tm, width), lambda i: (i, 0))


def _front_kernel(n_seg, carry, tiles_per_seq,
                  x_ref, hist_ref, pre_w_ref, w_u_ref, w_v_ref, w_z_ref, w_xbc_ref, w_dt_ref, w_ga_ref, w_gb_ref,
                  ln_w_ref, ln_b_ref, conv_w_ref, conv_b_ref, dt_bias_ref,
                  u_ref, v_ref, vf_ref, zs_ref, xs_ref, bm_ref, cm_ref, dt_ref, ga_ref, gb_ref, cstate_ref,
                  tail_ref):
    tm = x_ref.shape[0]
    seg_len = tm // n_seg
    x = x_ref[...]
    h = _rms_norm(x, pre_w_ref[...]).astype(BF16)

    u_ref[...] = _gelu_tanh(_dot(h, w_u_ref[...])).astype(BF16)

    v = _gelu_tanh(_dot(h, w_v_ref[...]))
    mu = jnp.mean(v, axis=-1, keepdims=True)
    vc = v - mu
    var = jnp.mean(vc * vc, axis=-1, keepdims=True)
    vn = vc * lax.rsqrt(var + EPS) * ln_w_ref[...] + ln_b_ref[...]
    v_ref[...] = vn.astype(BF16)
    vf_ref[...] = vn

    zs_ref[...] = _silu(_dot(h, w_z_ref[...])).astype(BF16)
    ga_ref[...] = _sigmoid(_dot(h, w_ga_ref[...])).astype(BF16)
    gb_ref[...] = _sigmoid(_dot(h, w_gb_ref[...])).astype(BF16)
    dt_ref[...] = _softplus(_dot(h, w_dt_ref[...]) + dt_bias_ref[...])

    raw = _dot(h, w_xbc_ref[...])
    cw = conv_w_ref[...]
    outs = []
    for s in range(n_seg):
        seg = raw[s * seg_len:(s + 1) * seg_len]
        if carry:
            first = (pl.program_id(0) % tiles_per_seq) == 0
            prev = jnp.where(first, hist_ref[0], tail_ref[...])
        else:
            prev = hist_ref[s]
        xp = jnp.concatenate([prev, seg], axis=0)
        acc = conv_b_ref[...] + seg * cw[CONV_WIDTH - 1:CONV_WIDTH]
        for k in range(1, CONV_WIDTH):
            shifted = pltpu.roll(xp, k, 0)[SUBLANES:]
            acc = acc + shifted * cw[CONV_WIDTH - 1 - k:CONV_WIDTH - k]
        outs.append(_silu(acc))
        cstate_ref[s] = seg[seg_len - SUBLANES:]
    if carry:
        tail_ref[...] = raw[tm - SUBLANES:]
    act = outs[0] if n_seg == 1 else jnp.concatenate(outs, axis=0)
    xs_ref[...] = act[:, :SSM_INNER].astype(BF16)
    bm_ref[...] = act[:, SSM_INNER:SSM_INNER + BC_WIDTH].astype(BF16)
    cm_ref[...] = act[:, SSM_INNER + BC_WIDTH:].astype(BF16)


def _front_call(x, hist, wts, *, tm, n_seg, carry, seq_len):
    t = x.shape[0]
    tiles_per_seq = seq_len // tm if carry else 1
    n_tiles = t // tm
    if carry:
        hist_spec = pl.BlockSpec((1, SUBLANES, CONV_DIM), lambda i: (i // tiles_per_seq, 0, 0))
        cstate_spec = pl.BlockSpec((1, SUBLANES, CONV_DIM), lambda i: (i // tiles_per_seq, 0, 0))
    else:
        hist_spec = pl.BlockSpec((n_seg, SUBLANES, CONV_DIM), lambda i: (i, 0, 0))
        cstate_spec = pl.BlockSpec((n_seg, SUBLANES, CONV_DIM), lambda i: (i, 0, 0))
    n_seq = hist.shape[0]
    in_specs = [_rows(tm, D_MODEL), hist_spec] + [_resident(w.shape) for w in wts]
    out_shape = (
        jax.ShapeDtypeStruct((t, GMLP_WIDTH), BF16),
        jax.ShapeDtypeStruct((t, GMLP_WIDTH), BF16),
        jax.ShapeDtypeStruct((t, GMLP_WIDTH), F32),
        jax.ShapeDtypeStruct((t, SSM_INNER), BF16),
        jax.ShapeDtypeStruct((t, SSM_INNER), BF16),
        jax.ShapeDtypeStruct((t, BC_WIDTH), BF16),
        jax.ShapeDtypeStruct((t, BC_WIDTH), BF16),
        jax.ShapeDtypeStruct((t, LANES), F32),
        jax.ShapeDtypeStruct((t, D_MODEL), BF16),
        jax.ShapeDtypeStruct((t, D_MODEL), BF16),
        jax.ShapeDtypeStruct((n_seq, SUBLANES, CONV_DIM), F32),
    )
    out_specs = (
        _rows(tm, GMLP_WIDTH), _rows(tm, GMLP_WIDTH), _rows(tm, GMLP_WIDTH), _rows(tm, SSM_INNER),
        _rows(tm, SSM_INNER), _rows(tm, BC_WIDTH), _rows(tm, BC_WIDTH), _rows(tm, LANES),
        _rows(tm, D_MODEL), _rows(tm, D_MODEL), cstate_spec,
    )
    return pl.pallas_call(
        functools.partial(_front_kernel, n_seg, carry, tiles_per_seq),
        grid=(n_tiles,),
        in_specs=in_specs,
        out_specs=out_specs,
        out_shape=out_shape,
        scratch_shapes=[pltpu.VMEM((SUBLANES, CONV_DIM), F32)],
        compiler_params=pltpu.CompilerParams(dimension_semantics=("arbitrary",), vmem_limit_bytes=VMEM_LIMIT),
        name="front_carry" if carry else "front_seg",
    )(x, hist, *wts)


def _mixer_kernel(n_seg, carry, tiles_per_seq,
                  x_ref, u_ref, v_ref, zs_ref, xs_ref, bm_ref, cm_ref, dt_ref, ga_ref, gb_ref, h0_ref,
                  wsp_ref, bsp_ref, w_a_ref, w_b_ref, w_out_ref, nrm_w_ref, post_w_ref, a_ref, dskip_ref,
                  exp_ref,
                  y_ref, hout_ref,
                  state_ref):
    tm = x_ref.shape[0]
    seg_len = tm // n_seg
    gw = SSM_GROUP_WIDTH

    v = v_ref[...]
    sp = [_dot(wsp_ref[g], v[:, g * GMLP_GROUP_DIM:(g + 1) * GMLP_GROUP_DIM]) for g in range(GMLP_GROUPS)]
    o_a = u_ref[...].astype(F32) * (jnp.concatenate(sp, axis=1) + bsp_ref[...])
    y_a = _dot(o_a.astype(BF16), w_a_ref[...])

    row = lax.broadcasted_iota(jnp.int32, (tm, tm), 0)
    col = lax.broadcasted_iota(jnp.int32, (tm, tm), 1)
    seg_shift = seg_len.bit_length() - 1
    causal = (row >= col) & ((row >> seg_shift) == (col >> seg_shift))
    dt = dt_ref[...]
    dta = dt * a_ref[...]
    ones_tri = jnp.where(causal, 1.0, 0.0).astype(BF16)
    acum3 = _dot(ones_tri, _split3(dta))
    acum = acum3[:, :LANES] + acum3[:, LANES:2 * LANES] + acum3[:, 2 * LANES:]
    acum_t = acum.T
    acum_x = _dot(_split3(acum), exp_ref[...])
    dt_x = _dot(_split3(dt), exp_ref[...])
    xs = xs_ref[...].astype(F32)
    xdt = xs * dt_x
    e_off = jnp.exp(acum_x)
    last_rows = [acum_x[(s + 1) * seg_len - 1:(s + 1) * seg_len] for s in range(n_seg)]
    if n_seg == 1:
        last_x = jnp.broadcast_to(last_rows[0], (tm, SSM_INNER))
    else:
        last_x = jnp.concatenate([jnp.broadcast_to(r, (seg_len, SSM_INNER)) for r in last_rows], axis=0)
    xw = (xdt * jnp.exp(last_x - acum_x)).astype(BF16)
    xdt_b = xdt.astype(BF16)
    lane = lax.broadcasted_iota(jnp.int32, (tm, LANES), 1)
    low_half = lane < SSM_HEAD_DIM

    bm = bm_ref[...]
    cm = cm_ref[...]
    if carry:
        first = (pl.program_id(0) % tiles_per_seq) == 0

        @pl.when(first)
        def _():
            for g in range(SSM_GROUPS):
                state_ref[g] = h0_ref[0, g * gw:(g + 1) * gw, :].T

    y_groups = []
    for g in range(SSM_GROUPS):
        cg = cm[:, g * SSM_STATE:(g + 1) * SSM_STATE]
        bg = bm[:, g * SSM_STATE:(g + 1) * SSM_STATE]
        cb = jnp.where(causal, _dot_nt(cg, bg), 0.0)
        y_pairs = []
        for j in range(SSM_HEADS_PER_GROUP // 2):
            lhs = []
            for hh in (2 * j, 2 * j + 1):
                hd = g * SSM_HEADS_PER_GROUP + hh
                seg = acum[:, hd:hd + 1] - acum_t[hd:hd + 1, :]
                lhs.append((jnp.exp(jnp.minimum(seg, 0.0)) * cb).astype(BF16))
            c0 = g * gw + j * LANES
            pair = xdt_b[:, c0:c0 + LANES]
            zero = jnp.zeros_like(pair)
            rhs = jnp.concatenate([jnp.where(low_half, pair, zero), jnp.where(low_half, zero, pair)], axis=0)
            y_pairs.append(_dot(jnp.concatenate(lhs, axis=1), rhs))
        y_diag = jnp.concatenate(y_pairs, axis=1)
        y_off = []
        for s in range(n_seg):
            r0, r1 = s * seg_len, (s + 1) * seg_len
            if carry:
                h_in = state_ref[g]
            else:
                h_in = h0_ref[s, g * gw:(g + 1) * gw, :].T
            y_off.append(_dot(cg[r0:r1], h_in.astype(BF16)))
            decay_end = e_off[r1 - 1:r1, g * gw:(g + 1) * gw]
            bt = bg[r0:r1].astype(F32).T.astype(BF16)
            h_new = h_in * decay_end + _dot(bt, xw[r0:r1, g * gw:(g + 1) * gw])
            if carry:
                state_ref[g] = h_new

                @pl.when((pl.program_id(0) % tiles_per_seq) == tiles_per_seq - 1)
                def _():
                    hout_ref[0, g * gw:(g + 1) * gw, :] = h_new.T
            else:
                hout_ref[s, g * gw:(g + 1) * gw, :] = h_new.T
        y_off = y_off[0] if n_seg == 1 else jnp.concatenate(y_off, axis=0)
        y_groups.append(y_diag + y_off * e_off[:, g * gw:(g + 1) * gw])
    y = jnp.concatenate(y_groups, axis=1) + dskip_ref[...] * xs

    gz = y * zs_ref[...].astype(F32)
    nw = nrm_w_ref[...]
    o_b = []
    for g in range(SSM_GROUPS):
        blk = gz[:, g * gw:(g + 1) * gw]
        blk = blk * lax.rsqrt(jnp.mean(blk * blk, axis=-1, keepdims=True) + EPS)
        o_b.append((blk * nw[:, g * gw:(g + 1) * gw]).astype(BF16))
    y_b = _dot(jnp.concatenate(o_b, axis=1), w_b_ref[...])

    merged = ga_ref[...].astype(F32) * y_a + gb_ref[...].astype(F32) * y_b
    mixed = _dot(merged.astype(BF16), w_out_ref[...])
    y_ref[...] = x_ref[...] + _rms_norm(mixed, post_w_ref[...])


def _mixer_call(x, front, h0, wts, *, tm, n_seg, carry, seq_len):
    t = x.shape[0]
    tiles_per_seq = seq_len // tm if carry else 1
    n_tiles = t // tm
    u, v, zs, xs, bm, cm, dt, ga, gb = front
    hp = SSM_HEADS * SSM_HEAD_DIM
    if carry:
        state_spec = pl.BlockSpec((1, hp, SSM_STATE), lambda i: (i // tiles_per_seq, 0, 0))
    else:
        state_spec = pl.BlockSpec((n_seg, hp, SSM_STATE), lambda i: (i, 0, 0))
    in_specs = [
        _rows(tm, D_MODEL), _rows(tm, GMLP_WIDTH), _rows(tm, GMLP_WIDTH), _rows(tm, SSM_INNER), _rows(tm, SSM_INNER),
        _rows(tm, BC_WIDTH), _rows(tm, BC_WIDTH), _rows(tm, LANES), _rows(tm, D_MODEL), _rows(tm, D_MODEL),
        state_spec,
    ] + [_resident(w.shape) for w in wts]
    return pl.pallas_call(
        functools.partial(_mixer_kernel, n_seg, carry, tiles_per_seq),
        grid=(n_tiles,),
        in_specs=in_specs,
        out_specs=(_rows(tm, D_MODEL), state_spec),
        out_shape=(jax.ShapeDtypeStruct((t, D_MODEL), F32), jax.ShapeDtypeStruct(h0.shape, F32)),
        scratch_shapes=[pltpu.VMEM((SSM_GROUPS, SSM_STATE, SSM_GROUP_WIDTH), F32)],
        compiler_params=pltpu.CompilerParams(dimension_semantics=("arbitrary",), vmem_limit_bytes=VMEM_LIMIT),
        name="mixer_carry" if carry else "mixer_seg",
    )(x, u, v, zs, xs, bm, cm, dt, ga, gb, h0, *wts)


def _ffn_kernel(x_ref, pre_w_ref, w_up_ref, w_down_ref, post_w_ref, y_ref):
    x = x_ref[...]
    h = _rms_norm(x, pre_w_ref[...]).astype(BF16)
    acc = None
    for c in range(FFN_HIDDEN // FFN_CHUNK):
        up = _dot(h, w_up_ref[:, c * FFN_CHUNK:(c + 1) * FFN_CHUNK])
        act = jnp.square(jnp.maximum(up, 0.0)).astype(BF16)
        part = _dot(act, w_down_ref[c * FFN_CHUNK:(c + 1) * FFN_CHUNK, :])
        acc = part if acc is None else acc + part
    y_ref[...] = x + _rms_norm(acc, post_w_ref[...])


def _ffn_call(x, wts, *, tm):
    t = x.shape[0]
    return pl.pallas_call(
        _ffn_kernel,
        grid=(t // tm,),
        in_specs=[_rows(tm, D_MODEL)] + [_resident(w.shape) for w in wts],
        out_specs=_rows(tm, D_MODEL),
        out_shape=jax.ShapeDtypeStruct((t, D_MODEL), F32),
        compiler_params=pltpu.CompilerParams(dimension_semantics=("arbitrary",), vmem_limit_bytes=VMEM_LIMIT),
        name="ffn",
    )(x, *wts)


def _spatial_tile(ws, bs, n, tm):
    pos = jnp.arange(n)
    mask = (pos[None, :] // CHUNK) <= (pos[:, None] // CHUNK)
    w = jnp.where(mask[None], ws[:, :n, :n], 0.0)
    reps = tm // n
    eye = jnp.eye(reps, dtype=w.dtype)
    tile = jnp.einsum("ab,gij->gaibj", eye, w).reshape(GMLP_GROUPS, tm, tm)
    bias = jnp.repeat(jnp.tile(bs[:, :n].T, (reps, 1)), GMLP_GROUP_DIM, axis=1)
    return tile.astype(BF16), bias.astype(F32)


def _layer(x, hist, h0, *, seq_len, carry, lw):
    (pre_mix_w, w_in, gmlp_ln_w, gmlp_ln_b, gmlp_ws, gmlp_bs, conv_w, conv_b, dt_bias, a_log, d_skip, ssm_norm_w,
     w_branch_a, w_branch_b, w_out, post_mix_w, pre_ffn_w, w_up, w_down, post_ffn_w) = lw
    n_seq = x.shape[0]
    t = n_seq * seq_len
    tm = TILE_M if carry else SEG_TILE_M
    n_seg = 1 if carry else tm // seq_len
    xf = x.reshape(t, D_MODEL)

    p1 = GMLP_WIDTH
    p2 = p1 + GMLP_WIDTH
    p3 = p2 + SSM_INNER
    p4 = p3 + CONV_DIM
    p5 = p4 + SSM_HEADS
    p6 = p5 + D_MODEL
    wb = w_in.astype(BF16)
    w_dt = jnp.pad(wb[:, p4:p5], ((0, 0), (0, LANES - SSM_HEADS)))
    row = lambda a: a.reshape(1, -1).astype(F32)
    front_w = (row(pre_mix_w), wb[:, :p1], wb[:, p1:p2], wb[:, p2:p3], wb[:, p3:p4], w_dt, wb[:, p5:p6], wb[:, p6:],
               row(gmlp_ln_w), row(gmlp_ln_b), conv_w.astype(F32), row(conv_b),
               jnp.pad(row(dt_bias), ((0, 0), (0, LANES - SSM_HEADS))))
    hist8 = jnp.pad(hist.astype(F32), ((0, 0), (SUBLANES - (CONV_WIDTH - 1), 0), (0, 0)))
    u, v, vf, zs, xs, bm, cm, dt, ga, gb, cstate = _front_call(xf, hist8, front_w, tm=tm, n_seg=n_seg,
                                                               carry=carry, seq_len=seq_len)

    n_pos = min(seq_len, GMLP_CHUNK)
    wsp, bsp = _spatial_tile(gmlp_ws, gmlp_bs, n_pos, tm)
    a_pad = jnp.pad(row(-jnp.exp(a_log.astype(F32))), ((0, 0), (0, LANES - SSM_HEADS)))
    expand = jnp.repeat(jnp.eye(LANES, SSM_HEADS, dtype=F32), SSM_HEAD_DIM, axis=1)
    expand3 = jnp.tile(expand, (3, 1)).astype(BF16)
    dskip_x = jnp.repeat(row(d_skip), SSM_HEAD_DIM, axis=1)
    mixer_w = (wsp, bsp, w_branch_a.astype(BF16), w_branch_b.astype(BF16), w_out.astype(BF16), row(ssm_norm_w),
               row(post_mix_w), a_pad, dskip_x, expand3)
    h0f = h0.astype(F32).reshape(n_seq, SSM_HEADS * SSM_HEAD_DIM, SSM_STATE)
    x1, h_last = _mixer_call(xf, (u, v, zs, xs, bm, cm, dt, ga, gb), h0f, mixer_w, tm=tm, n_seg=n_seg,
                             carry=carry, seq_len=seq_len)

    ffn_w = (row(pre_ffn_w), w_up.astype(BF16), w_down.astype(BF16), row(post_ffn_w))
    y = _ffn_call(x1, ffn_w, tm=tm)
    return (y.reshape(n_seq, seq_len, D_MODEL), cstate[:, SUBLANES - (CONV_WIDTH - 1):],
            h_last.reshape(n_seq, SSM_HEADS, SSM_HEAD_DIM, SSM_STATE).astype(h0.dtype),
            vf.reshape(n_seq, seq_len, GMLP_WIDTH))


def kernel(x_prompt, x_sample, cache_conv, state_ssm, pre_mix_w, w_in, gmlp_ln_w, gmlp_ln_b, gmlp_ws, gmlp_bs, conv_w, conv_b, dt_bias, a_log, d_skip, ssm_norm_w, w_branch_a, w_branch_b, w_out, post_mix_w, pre_ffn_w, w_up, w_down, post_ffn_w):
    depth = w_in.shape[0]
    yp, ys = x_prompt, x_sample
    n_p = x_prompt.shape[0]
    hist0 = jnp.zeros((n_p, CONV_WIDTH - 1, CONV_DIM), x_prompt.dtype)
    h00 = jnp.zeros((n_p, SSM_HEADS, SSM_HEAD_DIM, SSM_STATE), state_ssm.dtype)
    conv_p, ssm_p, conv_s, ssm_s, v_s = [], [], [], [], []
    for l in range(depth):
        lw = (pre_mix_w[l], w_in[l], gmlp_ln_w[l], gmlp_ln_b[l], gmlp_ws[l], gmlp_bs[l], conv_w[l], conv_b[l],
              dt_bias[l], a_log[l], d_skip[l], ssm_norm_w[l], w_branch_a[l], w_branch_b[l], w_out[l],
              post_mix_w[l], pre_ffn_w[l], w_up[l], w_down[l], post_ffn_w[l])
        yp, cp, sp, _ = _layer(yp, hist0, h00, seq_len=x_prompt.shape[1], carry=True, lw=lw)
        ys, cs, ss, vs = _layer(ys, cache_conv[l], state_ssm[l], seq_len=x_sample.shape[1], carry=False, lw=lw)
        conv_p.append(cp)
        ssm_p.append(sp)
        conv_s.append(cs)
        ssm_s.append(ss)
        v_s.append(vs)
    return (yp, ys, jnp.stack(conv_p), jnp.stack(ssm_p), jnp.stack(conv_s), jnp.stack(ssm_s), jnp.stack(v_s))
```

```python
import functools

import jax
import jax.numpy as jnp
from jax import lax
from jax.experimental import pallas as pl
from jax.experimental.pallas import tpu as pltpu

D_MODEL = 1024
CHUNK = 64
GMLP_CHUNK = 128
GMLP_WIDTH = 1024
GMLP_GROUPS = 8
GMLP_GROUP_DIM = GMLP_WIDTH // GMLP_GROUPS
SSM_INNER = 2 * D_MODEL
SSM_HEAD_DIM = 64
SSM_HEADS = SSM_INNER // SSM_HEAD_DIM
SSM_GROUPS = 4
SSM_HEADS_PER_GROUP = SSM_HEADS // SSM_GROUPS
SSM_STATE = 128
SSM_GROUP_WIDTH = SSM_INNER // SSM_GROUPS
BC_WIDTH = SSM_GROUPS * SSM_STATE
CONV_WIDTH = 4
CONV_DIM = SSM_INNER + 2 * BC_WIDTH
FFN_HIDDEN = 4 * D_MODEL
EPS = 1e-6
LOG2E = 1.4426950408889634

LANES = 128
SUBLANES = 8
TILE_M = 256
SEG_TILE_M = 128
FFN_TILE_M = 512
FFN_CHUNK = 1024
FRONT_CHUNK = 512
VMEM_LIMIT = 56 * 1024 * 1024

BF16 = jnp.bfloat16
F32 = jnp.float32

COL_U = 0
COL_V = COL_U + GMLP_WIDTH
COL_Z = COL_V + GMLP_WIDTH
COL_XBC = COL_Z + SSM_INNER
COL_GA = COL_XBC + CONV_DIM
COL_GB = COL_GA + D_MODEL
COL_DT = COL_GB + D_MODEL
IN_COLS = COL_DT + LANES


def _dot(a, b):
    return jnp.dot(a, b, preferred_element_type=F32)


def _dot_nt(a, b):
    return lax.dot_general(a, b, (((1,), (1,)), ((), ())), preferred_element_type=F32)


def _rms_norm(x, w):
    return x * lax.rsqrt(jnp.mean(x * x, axis=-1, keepdims=True) + EPS) * w


def _sigmoid(x):
    return 1.0 / (1.0 + jnp.exp2(x * (-LOG2E)))


def _silu(x):
    return x * _sigmoid(x)


def _gelu_tanh(x):
    c = 0.7978845608028654
    return 0.5 * x * (1.0 + jnp.tanh(c * (x + 0.044715 * (x * x * x))))


def _softplus(x):
    return jnp.maximum(x, 0.0) + jnp.log(1.0 + jnp.exp(-jnp.abs(x)))


def _split3(x):
    hi = x.astype(BF16)
    r1 = x - hi.astype(F32)
    mid = r1.astype(BF16)
    lo = (r1 - mid.astype(F32)).astype(BF16)
    return jnp.concatenate([hi, mid, lo], axis=1)


def _resident(shape):
    nd = len(shape)
    return pl.BlockSpec(shape, lambda i: (0,) * nd, pipeline_mode=pl.Buffered(1))


def _rows(tm, width):
    return pl.BlockSpec((tm, width), lambda i: (i, 0))


def _front_kernel(n_seg, carry, tiles_per_seq,
                  x_ref, hist_ref, pre_w_ref, w_ref, ln_w_ref, ln_b_ref, conv_w_ref, conv_b_ref, dt_bias_ref,
                  u_ref, v_ref, zs_ref, xs_ref, bm_ref, cm_ref, dt_ref, ga_ref, gb_ref, cstate_ref, *rest):
    tail_ref = rest[-1]
    vf_ref = rest[0] if len(rest) == 2 else None
    tm = x_ref.shape[0]
    seg_len = tm // n_seg
    cc = FRONT_CHUNK
    x = x_ref[...]
    h = _rms_norm(x, pre_w_ref[...]).astype(BF16)
    if carry:
        first = (pl.program_id(0) % tiles_per_seq) == 0

    def conv_chunk(c):
        cols = slice(c * cc, (c + 1) * cc)
        raw = _dot(h, w_ref[:, COL_XBC + c * cc:COL_XBC + (c + 1) * cc])
        cw = conv_w_ref[:, cols]
        outs = []
        for s in range(n_seg):
            seg = raw[s * seg_len:(s + 1) * seg_len]
            if carry:
                prev = jnp.where(first, hist_ref[0, :, cols], tail_ref[:, cols])
            else:
                prev = hist_ref[s, :, cols]
            xp = jnp.concatenate([prev, seg], axis=0)
            acc = conv_b_ref[:, cols] + seg * cw[CONV_WIDTH - 1:CONV_WIDTH]
            for k in range(1, CONV_WIDTH):
                shifted = pltpu.roll(xp, k, 0)[SUBLANES:]
                acc = acc + shifted * cw[CONV_WIDTH - 1 - k:CONV_WIDTH - k]
            outs.append(_silu(acc))
            cstate_ref[s, :, cols] = seg[seg_len - SUBLANES:]
        if carry:
            tail_ref[:, cols] = raw[tm - SUBLANES:]
        act = (outs[0] if n_seg == 1 else jnp.concatenate(outs, axis=0)).astype(BF16)
        if c < SSM_INNER // cc:
            xs_ref[:, cols] = act
        elif c < (SSM_INNER + BC_WIDTH) // cc:
            bm_ref[:, c * cc - SSM_INNER:(c + 1) * cc - SSM_INNER] = act
        else:
            cm_ref[:, c * cc - SSM_INNER - BC_WIDTH:(c + 1) * cc - SSM_INNER - BC_WIDTH] = act

    def proj(col0, c):
        return _dot(h, w_ref[:, col0 + c * cc:col0 + (c + 1) * cc])

    for c in range(CONV_DIM // cc):
        conv_chunk(c)
    for c in range(D_MODEL // cc):
        ga_ref[:, c * cc:(c + 1) * cc] = _sigmoid(proj(COL_GA, c)).astype(BF16)
        gb_ref[:, c * cc:(c + 1) * cc] = _sigmoid(proj(COL_GB, c)).astype(BF16)
    for c in range(SSM_INNER // cc):
        zs_ref[:, c * cc:(c + 1) * cc] = _silu(proj(COL_Z, c)).astype(BF16)
    for c in range(GMLP_WIDTH // cc):
        u_ref[:, c * cc:(c + 1) * cc] = _gelu_tanh(proj(COL_U, c)).astype(BF16)

    v = jnp.concatenate([_gelu_tanh(proj(COL_V, c)) for c in range(GMLP_WIDTH // cc)], axis=1)
    mu = jnp.mean(v, axis=-1, keepdims=True)
    vc = v - mu
    var = jnp.mean(vc * vc, axis=-1, keepdims=True)
    vn = vc * lax.rsqrt(var + EPS) * ln_w_ref[...] + ln_b_ref[...]
    v_ref[...] = vn.astype(BF16)
    if vf_ref is not None:
        vf_ref[...] = vn
    dt_ref[...] = _softplus(_dot(h, w_ref[:, COL_DT:COL_DT + LANES]) + dt_bias_ref[...])


def _front_call(x, hist, wts, *, tm, n_seg, carry, seq_len, emit_vf):
    t = x.shape[0]
    tiles_per_seq = seq_len // tm if carry else 1
    n_tiles = t // tm
    if carry:
        hist_spec = pl.BlockSpec((1, SUBLANES, CONV_DIM), lambda i: (i // tiles_per_seq, 0, 0))
        cstate_spec = pl.BlockSpec((1, SUBLANES, CONV_DIM), lambda i: (i // tiles_per_seq, 0, 0))
    else:
        hist_spec = pl.BlockSpec((n_seg, SUBLANES, CONV_DIM), lambda i: (i, 0, 0))
        cstate_spec = pl.BlockSpec((n_seg, SUBLANES, CONV_DIM), lambda i: (i, 0, 0))
    n_seq = hist.shape[0]
    in_specs = [_rows(tm, D_MODEL), hist_spec] + [_resident(w.shape) for w in wts]
    out_shape = [
        jax.ShapeDtypeStruct((t, GMLP_WIDTH), BF16),
        jax.ShapeDtypeStruct((t, GMLP_WIDTH), BF16),
        jax.ShapeDtypeStruct((t, SSM_INNER), BF16),
        jax.ShapeDtypeStruct((t, SSM_INNER), BF16),
        jax.ShapeDtypeStruct((t, BC_WIDTH), BF16),
        jax.ShapeDtypeStruct((t, BC_WIDTH), BF16),
        jax.ShapeDtypeStruct((t, LANES), F32),
        jax.ShapeDtypeStruct((t, D_MODEL), BF16),
        jax.ShapeDtypeStruct((t, D_MODEL), BF16),
        jax.ShapeDtypeStruct((n_seq, SUBLANES, CONV_DIM), F32),
    ]
    out_specs = [
        _rows(tm, GMLP_WIDTH), _rows(tm, GMLP_WIDTH), _rows(tm, SSM_INNER),
        _rows(tm, SSM_INNER), _rows(tm, BC_WIDTH), _rows(tm, BC_WIDTH), _rows(tm, LANES),
        _rows(tm, D_MODEL), _rows(tm, D_MODEL), cstate_spec,
    ]
    if emit_vf:
        out_shape.append(jax.ShapeDtypeStruct((t, GMLP_WIDTH), F32))
        out_specs.append(_rows(tm, GMLP_WIDTH))
    return pl.pallas_call(
        functools.partial(_front_kernel, n_seg, carry, tiles_per_seq),
        grid=(n_tiles,),
        in_specs=in_specs,
        out_specs=out_specs,
        out_shape=out_shape,
        scratch_shapes=[pltpu.VMEM((SUBLANES, CONV_DIM), F32)],
        compiler_params=pltpu.CompilerParams(dimension_semantics=("arbitrary",), vmem_limit_bytes=VMEM_LIMIT),
        name="front_carry" if carry else "front_seg",
    )(x, hist, *wts)


def _mixer_kernel(n_seg, carry, tiles_per_seq,
                  x_ref, u_ref, v_ref, zs_ref, xs_ref, bm_ref, cm_ref, dt_ref, ga_ref, gb_ref, h0_ref,
                  wsp_ref, bsp_ref, w_a_ref, w_b_ref, w_out_ref, nrm_w_ref, post_w_ref, a_ref, dskip_ref,
                  exp_ref,
                  y_ref, hout_ref,
                  state_ref):
    tm = x_ref.shape[0]
    seg_len = tm // n_seg
    gw = SSM_GROUP_WIDTH

    v = v_ref[...]
    sp = [_dot(wsp_ref[g], v[:, g * GMLP_GROUP_DIM:(g + 1) * GMLP_GROUP_DIM]) for g in range(GMLP_GROUPS)]
    o_a = u_ref[...].astype(F32) * (jnp.concatenate(sp, axis=1) + bsp_ref[...])
    y_a = _dot(o_a.astype(BF16), w_a_ref[...])

    row = lax.broadcasted_iota(jnp.int32, (tm, tm), 0)
    col = lax.broadcasted_iota(jnp.int32, (tm, tm), 1)
    seg_shift = seg_len.bit_length() - 1
    causal = (row >= col) & ((row >> seg_shift) == (col >> seg_shift))
    dt = dt_ref[...]
    dta = dt * a_ref[...]
    ones_tri = jnp.where(causal, 1.0, 0.0).astype(BF16)
    acum3 = _dot(ones_tri, _split3(dta))
    acum = acum3[:, :LANES] + acum3[:, LANES:2 * LANES] + acum3[:, 2 * LANES:]
    acum2 = acum * LOG2E
    acum2_t = acum2.T
    acum_x = _dot(_split3(acum), exp_ref[...])
    dt_x = _dot(_split3(dt), exp_ref[...])
    xs = xs_ref[...].astype(F32)
    xdt = xs * dt_x
    e_off = jnp.exp(acum_x)
    last_rows = [acum_x[(s + 1) * seg_len - 1:(s + 1) * seg_len] for s in range(n_seg)]
    if n_seg == 1:
        last_x = jnp.broadcast_to(last_rows[0], (tm, SSM_INNER))
    else:
        last_x = jnp.concatenate([jnp.broadcast_to(r, (seg_len, SSM_INNER)) for r in last_rows], axis=0)
    xw = (xdt * jnp.exp(last_x - acum_x)).astype(BF16)
    xdt_b = xdt.astype(BF16)
    lane = lax.broadcasted_iota(jnp.int32, (tm, LANES), 1)
    low_half = lane < SSM_HEAD_DIM

    bm = bm_ref[...]
    cm = cm_ref[...]
    if carry:
        first = (pl.program_id(0) % tiles_per_seq) == 0

        @pl.when(first)
        def _():
            for g in range(SSM_GROUPS):
                state_ref[g] = h0_ref[0, g * gw:(g + 1) * gw, :].T

    y_groups = []
    for g in range(SSM_GROUPS):
        cg = cm[:, g * SSM_STATE:(g + 1) * SSM_STATE]
        bg = bm[:, g * SSM_STATE:(g + 1) * SSM_STATE]
        cb = jnp.where(causal, _dot_nt(cg, bg), 0.0)
        y_pairs = []
        for j in range(SSM_HEADS_PER_GROUP // 2):
            lhs = []
            for hh in (2 * j, 2 * j + 1):
                hd = g * SSM_HEADS_PER_GROUP + hh
                seg = acum2[:, hd:hd + 1] - acum2_t[hd:hd + 1, :]
                lhs.append((jnp.exp2(jnp.minimum(seg, 0.0)) * cb).astype(BF16))
            c0 = g * gw + j * LANES
            pair = xdt_b[:, c0:c0 + LANES]
            zero = jnp.zeros_like(pair)
            rhs = jnp.concatenate([jnp.where(low_half, pair, zero), jnp.where(low_half, zero, pair)], axis=0)
            y_pairs.append(_dot(jnp.concatenate(lhs, axis=1), rhs))
        y_diag = jnp.concatenate(y_pairs, axis=1)
        y_off = []
        for s in range(n_seg):
            r0, r1 = s * seg_len, (s + 1) * seg_len
            if carry:
                h_in = state_ref[g]
            else:
                h_in = h0_ref[s, g * gw:(g + 1) * gw, :].T
            y_off.append(_dot(cg[r0:r1], h_in.astype(BF16)))
            decay_end = e_off[r1 - 1:r1, g * gw:(g + 1) * gw]
            bt = bg[r0:r1].astype(F32).T.astype(BF16)
            h_new = h_in * decay_end + _dot(bt, xw[r0:r1, g * gw:(g + 1) * gw])
            if carry:
                state_ref[g] = h_new
            else:
                hout_ref[s, g * gw:(g + 1) * gw, :] = h_new.T
        y_off = y_off[0] if n_seg == 1 else jnp.concatenate(y_off, axis=0)
        y_groups.append(y_diag + y_off * e_off[:, g * gw:(g + 1) * gw])
    y = jnp.concatenate(y_groups, axis=1) + dskip_ref[...] * xs

    gz = y * zs_ref[...].astype(F32)
    nw = nrm_w_ref[...]
    o_b = []
    for g in range(SSM_GROUPS):
        blk = gz[:, g * gw:(g + 1) * gw]
        blk = blk * lax.rsqrt(jnp.mean(blk * blk, axis=-1, keepdims=True) + EPS)
        o_b.append((blk * nw[:, g * gw:(g + 1) * gw]).astype(BF16))
    y_b = _dot(jnp.concatenate(o_b, axis=1), w_b_ref[...])

    merged = ga_ref[...].astype(F32) * y_a + gb_ref[...].astype(F32) * y_b
    mixed = _dot(merged.astype(BF16), w_out_ref[...])
    y_ref[...] = x_ref[...] + _rms_norm(mixed, post_w_ref[...])

    if carry:
        @pl.when((pl.program_id(0) % tiles_per_seq) == tiles_per_seq - 1)
        def _():
            for g in range(SSM_GROUPS):
                hout_ref[0, g * gw:(g + 1) * gw, :] = state_ref[g].T


def _mixer_call(x, front, h0, wts, *, tm, n_seg, carry, seq_len):
    t = x.shape[0]
    tiles_per_seq = seq_len // tm if carry else 1
    n_tiles = t // tm
    u, v, zs, xs, bm, cm, dt, ga, gb = front
    hp = SSM_HEADS * SSM_HEAD_DIM
    if carry:
        state_spec = pl.BlockSpec((1, hp, SSM_STATE), lambda i: (i // tiles_per_seq, 0, 0))
    else:
        state_spec = pl.BlockSpec((n_seg, hp, SSM_STATE), lambda i: (i, 0, 0))
    in_specs = [
        _rows(tm, D_MODEL), _rows(tm, GMLP_WIDTH), _rows(tm, GMLP_WIDTH), _rows(tm, SSM_INNER), _rows(tm, SSM_INNER),
        _rows(tm, BC_WIDTH), _rows(tm, BC_WIDTH), _rows(tm, LANES), _rows(tm, D_MODEL), _rows(tm, D_MODEL),
        state_spec,
    ] + [_resident(w.shape) for w in wts]
    return pl.pallas_call(
        functools.partial(_mixer_kernel, n_seg, carry, tiles_per_seq),
        grid=(n_tiles,),
        in_specs=in_specs,
        out_specs=(_rows(tm, D_MODEL), state_spec),
        out_shape=(jax.ShapeDtypeStruct((t, D_MODEL), F32), jax.ShapeDtypeStruct(h0.shape, F32)),
        scratch_shapes=[pltpu.VMEM((SSM_GROUPS, SSM_STATE, SSM_GROUP_WIDTH), F32)],
        compiler_params=pltpu.CompilerParams(dimension_semantics=("arbitrary",), vmem_limit_bytes=VMEM_LIMIT),
        name="mixer_carry" if carry else "mixer_seg",
    )(x, u, v, zs, xs, bm, cm, dt, ga, gb, h0, *wts)


def _ffn_kernel(x_ref, pre_w_ref, w_up_ref, w_down_ref, post_w_ref, y_ref):
    x = x_ref[...]
    h = _rms_norm(x, pre_w_ref[...]).astype(BF16)
    acc = None
    for c in range(FFN_HIDDEN // FFN_CHUNK):
        up = _dot(h, w_up_ref[:, c * FFN_CHUNK:(c + 1) * FFN_CHUNK])
        act = jnp.square(jnp.maximum(up, 0.0)).astype(BF16)
        part = _dot(act, w_down_ref[c * FFN_CHUNK:(c + 1) * FFN_CHUNK, :])
        acc = part if acc is None else acc + part
    y_ref[...] = x + _rms_norm(acc, post_w_ref[...])


def _ffn_call(x, wts, *, tm):
    t = x.shape[0]
    return pl.pallas_call(
        _ffn_kernel,
        grid=(t // tm,),
        in_specs=[_rows(tm, D_MODEL)] + [_resident(w.shape) for w in wts],
        out_specs=_rows(tm, D_MODEL),
        out_shape=jax.ShapeDtypeStruct((t, D_MODEL), F32),
        compiler_params=pltpu.CompilerParams(dimension_semantics=("arbitrary",), vmem_limit_bytes=VMEM_LIMIT),
        name="ffn",
    )(x, *wts)


def _spatial_tile(ws, bs, n, tm):
    pos = jnp.arange(n)
    mask = (pos[None, :] // CHUNK) <= (pos[:, None] // CHUNK)
    w = jnp.where(mask[None], ws[:, :n, :n], 0.0)
    reps = tm // n
    eye = jnp.eye(reps, dtype=w.dtype)
    tile = jnp.einsum("ab,gij->gaibj", eye, w).reshape(GMLP_GROUPS, tm, tm)
    bias = jnp.repeat(jnp.tile(bs[:, :n].T, (reps, 1)), GMLP_GROUP_DIM, axis=1)
    return tile.astype(BF16), bias.astype(F32)


def _layer(x, hist, h0, *, seq_len, carry, want_v, lw):
    (pre_mix_w, w_in, gmlp_ln_w, gmlp_ln_b, gmlp_ws, gmlp_bs, conv_w, conv_b, dt_bias, a_log, d_skip, ssm_norm_w,
     w_branch_a, w_branch_b, w_out, post_mix_w, pre_ffn_w, w_up, w_down, post_ffn_w) = lw
    n_seq = x.shape[0]
    t = n_seq * seq_len
    tm = TILE_M if carry else SEG_TILE_M
    n_seg = 1 if carry else tm // seq_len
    xf = x.reshape(t, D_MODEL)

    p1 = GMLP_WIDTH
    p2 = p1 + GMLP_WIDTH
    p3 = p2 + SSM_INNER
    p4 = p3 + CONV_DIM
    p5 = p4 + SSM_HEADS
    w_cat = jnp.concatenate([w_in[:, :p4], w_in[:, p5:], w_in[:, p4:p5],
                             jnp.zeros((D_MODEL, LANES - SSM_HEADS), w_in.dtype)], axis=1).astype(BF16)
    row = lambda a: a.reshape(1, -1).astype(F32)
    front_w = (row(pre_mix_w), w_cat, row(gmlp_ln_w), row(gmlp_ln_b), conv_w.astype(F32), row(conv_b),
               jnp.pad(row(dt_bias), ((0, 0), (0, LANES - SSM_HEADS))))
    hist8 = jnp.pad(hist.astype(F32), ((0, 0), (SUBLANES - (CONV_WIDTH - 1), 0), (0, 0)))
    u, v, zs, xs, bm, cm, dt, ga, gb, cstate, *vf = _front_call(xf, hist8, front_w, tm=tm, n_seg=n_seg, carry=carry,
                                                                seq_len=seq_len, emit_vf=want_v)

    n_pos = min(seq_len, GMLP_CHUNK)
    wsp, bsp = _spatial_tile(gmlp_ws, gmlp_bs, n_pos, tm)
    a_pad = jnp.pad(row(-jnp.exp(a_log.astype(F32))), ((0, 0), (0, LANES - SSM_HEADS)))
    expand = jnp.repeat(jnp.eye(LANES, SSM_HEADS, dtype=F32), SSM_HEAD_DIM, axis=1)
    expand3 = jnp.tile(expand, (3, 1)).astype(BF16)
    dskip_x = jnp.repeat(row(d_skip), SSM_HEAD_DIM, axis=1)
    mixer_w = (wsp, bsp, w_branch_a.astype(BF16), w_branch_b.astype(BF16), w_out.astype(BF16), row(ssm_norm_w),
               row(post_mix_w), a_pad, dskip_x, expand3)
    h0f = h0.astype(F32).reshape(n_seq, SSM_HEADS * SSM_HEAD_DIM, SSM_STATE)
    x1, h_last = _mixer_call(xf, (u, v, zs, xs, bm, cm, dt, ga, gb), h0f, mixer_w, tm=tm, n_seg=n_seg,
                             carry=carry, seq_len=seq_len)

    ffn_w = (row(pre_ffn_w), w_up.astype(BF16), w_down.astype(BF16), row(post_ffn_w))
    y = _ffn_call(x1, ffn_w, tm=FFN_TILE_M)
    return (y.reshape(n_seq, seq_len, D_MODEL), cstate[:, SUBLANES - (CONV_WIDTH - 1):],
            h_last.reshape(n_seq, SSM_HEADS, SSM_HEAD_DIM, SSM_STATE).astype(h0.dtype),
            vf[0].reshape(n_seq, seq_len, GMLP_WIDTH) if want_v else None)


def kernel(x_prompt, x_sample, cache_conv, state_ssm, pre_mix_w, w_in, gmlp_ln_w, gmlp_ln_b, gmlp_ws, gmlp_bs, conv_w, conv_b, dt_bias, a_log, d_skip, ssm_norm_w, w_branch_a, w_branch_b, w_out, post_mix_w, pre_ffn_w, w_up, w_down, post_ffn_w):
    depth = w_in.shape[0]
    yp, ys = x_prompt, x_sample
    n_p = x_prompt.shape[0]
    hist0 = jnp.zeros((n_p, CONV_WIDTH - 1, CONV_DIM), x_prompt.dtype)
    h00 = jnp.zeros((n_p, SSM_HEADS, SSM_HEAD_DIM, SSM_STATE), state_ssm.dtype)
    conv_p, ssm_p, conv_s, ssm_s, v_s = [], [], [], [], []
    for l in range(depth):
        lw = (pre_mix_w[l], w_in[l], gmlp_ln_w[l], gmlp_ln_b[l], gmlp_ws[l], gmlp_bs[l], conv_w[l], conv_b[l],
              dt_bias[l], a_log[l], d_skip[l], ssm_norm_w[l], w_branch_a[l], w_branch_b[l], w_out[l],
              post_mix_w[l], pre_ffn_w[l], w_up[l], w_down[l], post_ffn_w[l])
        yp, cp, sp, _ = _layer(yp, hist0, h00, seq_len=x_prompt.shape[1], carry=True, want_v=False, lw=lw)
        ys, cs, ss, vs = _layer(ys, cache_conv[l], state_ssm[l], seq_len=x_sample.shape[1], carry=False, want_v=True,
                                lw=lw)
        conv_p.append(cp)
        ssm_p.append(sp)
        conv_s.append(cs)
        ssm_s.append(ss)
        v_s.append(vs)
    return (yp, ys, jnp.stack(conv_p), jnp.stack(ssm_p), jnp.stack(conv_s), jnp.stack(ssm_s), jnp.stack(v_s))
```

```python
import functools

import jax
import jax.numpy as jnp
from jax import lax
from jax.experimental import pallas as pl
from jax.experimental.pallas import tpu as pltpu

D_MODEL = 1024
CHUNK = 64
GMLP_CHUNK = 128
GMLP_WIDTH = 1024
GMLP_GROUPS = 8
GMLP_GROUP_DIM = GMLP_WIDTH // GMLP_GROUPS
SSM_INNER = 2 * D_MODEL
SSM_HEAD_DIM = 64
SSM_HEADS = SSM_INNER // SSM_HEAD_DIM
SSM_GROUPS = 4
SSM_HEADS_PER_GROUP = SSM_HEADS // SSM_GROUPS
SSM_STATE = 128
SSM_GROUP_WIDTH = SSM_INNER // SSM_GROUPS
BC_WIDTH = SSM_GROUPS * SSM_STATE
CONV_WIDTH = 4
CONV_DIM = SSM_INNER + 2 * BC_WIDTH
FFN_HIDDEN = 4 * D_MODEL
EPS = 1e-6
LOG2E = 1.4426950408889634

LANES = 128
SUBLANES = 8
TILE_M = 256
SEG_TILE_M = 128
FFN_TILE_M = 512
FFN_CHUNK = 1024
SSD_BLOCK = 128
FRONT_CHUNK = 512
VMEM_LIMIT = 56 * 1024 * 1024

BF16 = jnp.bfloat16
F32 = jnp.float32

COL_U = 0
COL_V = COL_U + GMLP_WIDTH
COL_Z = COL_V + GMLP_WIDTH
COL_XBC = COL_Z + SSM_INNER
MAIN_COLS = COL_XBC + CONV_DIM
TAIL_GA = 0
TAIL_GB = TAIL_GA + D_MODEL
TAIL_DT = TAIL_GB + D_MODEL


def _dot(a, b):
    return jnp.dot(a, b, preferred_element_type=F32)


def _dot_nt(a, b):
    return lax.dot_general(a, b, (((1,), (1,)), ((), ())), preferred_element_type=F32)


def _rms_norm(x, w):
    return x * lax.rsqrt(jnp.mean(x * x, axis=-1, keepdims=True) + EPS) * w


def _sigmoid(x):
    return 1.0 / (1.0 + jnp.exp2(x * (-LOG2E)))


def _silu(x):
    return x * _sigmoid(x)


def _gelu_tanh(x):
    c = 0.7978845608028654
    return 0.5 * x * (1.0 + jnp.tanh(c * (x + 0.044715 * (x * x * x))))


def _softplus(x):
    return jnp.maximum(x, 0.0) + jnp.log(1.0 + jnp.exp(-jnp.abs(x)))


def _split3(x):
    hi = x.astype(BF16)
    r1 = x - hi.astype(F32)
    mid = r1.astype(BF16)
    lo = (r1 - mid.astype(F32)).astype(BF16)
    return jnp.concatenate([hi, mid, lo], axis=1)


def _resident(shape):
    nd = len(shape)
    return pl.BlockSpec(shape, lambda i: (0,) * nd, pipeline_mode=pl.Buffered(1))


def _rows(tm, width):
    return pl.BlockSpec((tm, width), lambda i: (i, 0))


def _front_kernel(n_seg, carry, tiles_per_seq,
                  x_ref, hist_ref, pre_w_ref, w_ref, w_tail_ref, ln_w_ref, ln_b_ref, conv_w_ref, conv_b_ref, dt_bias_ref,
                  u_ref, v_ref, zs_ref, xs_ref, bm_ref, cm_ref, dt_ref, ga_ref, gb_ref, cstate_ref, *rest):
    tail_ref = rest[-1]
    vf_ref = rest[0] if len(rest) == 2 else None
    tm = x_ref.shape[0]
    seg_len = tm // n_seg
    cc = FRONT_CHUNK
    x = x_ref[...]
    h = _rms_norm(x, pre_w_ref[...]).astype(BF16)
    if carry:
        first = (pl.program_id(0) % tiles_per_seq) == 0

    def conv_chunk(c):
        cols = slice(c * cc, (c + 1) * cc)
        raw = _dot(h, w_ref[:, COL_XBC + c * cc:COL_XBC + (c + 1) * cc])
        cw = conv_w_ref[:, cols]
        outs = []
        for s in range(n_seg):
            seg = raw[s * seg_len:(s + 1) * seg_len]
            if carry:
                prev = jnp.where(first, hist_ref[0, :, cols], tail_ref[:, cols])
            else:
                prev = hist_ref[s, :, cols]
            xp = jnp.concatenate([prev, seg], axis=0)
            acc = conv_b_ref[:, cols] + seg * cw[CONV_WIDTH - 1:CONV_WIDTH]
            for k in range(1, CONV_WIDTH):
                shifted = pltpu.roll(xp, k, 0)[SUBLANES:]
                acc = acc + shifted * cw[CONV_WIDTH - 1 - k:CONV_WIDTH - k]
            outs.append(_silu(acc))
            cstate_ref[s, :, cols] = seg[seg_len - SUBLANES:]
        if carry:
            tail_ref[:, cols] = raw[tm - SUBLANES:]
        act = (outs[0] if n_seg == 1 else jnp.concatenate(outs, axis=0)).astype(BF16)
        if c < SSM_INNER // cc:
            xs_ref[:, cols] = act
        elif c < (SSM_INNER + BC_WIDTH) // cc:
            bm_ref[:, c * cc - SSM_INNER:(c + 1) * cc - SSM_INNER] = act
        else:
            cm_ref[:, c * cc - SSM_INNER - BC_WIDTH:(c + 1) * cc - SSM_INNER - BC_WIDTH] = act

    def proj(col0, c, w=w_ref):
        return _dot(h, w[:, col0 + c * cc:col0 + (c + 1) * cc])

    for c in range(CONV_DIM // cc):
        conv_chunk(c)
    for c in range(D_MODEL // cc):
        ga_ref[:, c * cc:(c + 1) * cc] = _sigmoid(proj(TAIL_GA, c, w_tail_ref)).astype(BF16)
        gb_ref[:, c * cc:(c + 1) * cc] = _sigmoid(proj(TAIL_GB, c, w_tail_ref)).astype(BF16)
    for c in range(SSM_INNER // cc):
        zs_ref[:, c * cc:(c + 1) * cc] = _silu(proj(COL_Z, c)).astype(BF16)
    for c in range(GMLP_WIDTH // cc):
        u_ref[:, c * cc:(c + 1) * cc] = _gelu_tanh(proj(COL_U, c)).astype(BF16)

    v = jnp.concatenate([_gelu_tanh(proj(COL_V, c)) for c in range(GMLP_WIDTH // cc)], axis=1)
    mu = jnp.mean(v, axis=-1, keepdims=True)
    vc = v - mu
    var = jnp.mean(vc * vc, axis=-1, keepdims=True)
    vn = vc * lax.rsqrt(var + EPS) * ln_w_ref[...] + ln_b_ref[...]
    v_ref[...] = vn.astype(BF16)
    if vf_ref is not None:
        vf_ref[...] = vn
    dt_ref[...] = _softplus(_dot(h, w_tail_ref[:, TAIL_DT:TAIL_DT + LANES]) + dt_bias_ref[...])


def _front_call(x, hist, wts, *, tm, n_seg, carry, seq_len, emit_vf):
    t = x.shape[0]
    tiles_per_seq = seq_len // tm if carry else 1
    n_tiles = t // tm
    if carry:
        hist_spec = pl.BlockSpec((1, SUBLANES, CONV_DIM), lambda i: (i // tiles_per_seq, 0, 0))
        cstate_spec = pl.BlockSpec((1, SUBLANES, CONV_DIM), lambda i: (i // tiles_per_seq, 0, 0))
    else:
        hist_spec = pl.BlockSpec((n_seg, SUBLANES, CONV_DIM), lambda i: (i, 0, 0))
        cstate_spec = pl.BlockSpec((n_seg, SUBLANES, CONV_DIM), lambda i: (i, 0, 0))
    n_seq = hist.shape[0]
    w_specs = [_resident(w.shape) for w in wts]
    w_specs[1] = _resident((D_MODEL, MAIN_COLS))
    in_specs = [_rows(tm, D_MODEL), hist_spec] + w_specs
    out_shape = [
        jax.ShapeDtypeStruct((t, GMLP_WIDTH), BF16),
        jax.ShapeDtypeStruct((t, GMLP_WIDTH), BF16),
        jax.ShapeDtypeStruct((t, SSM_INNER), BF16),
        jax.ShapeDtypeStruct((t, SSM_INNER), BF16),
        jax.ShapeDtypeStruct((t, BC_WIDTH), BF16),
        jax.ShapeDtypeStruct((t, BC_WIDTH), BF16),
        jax.ShapeDtypeStruct((t, LANES), F32),
        jax.ShapeDtypeStruct((t, D_MODEL), BF16),
        jax.ShapeDtypeStruct((t, D_MODEL), BF16),
        jax.ShapeDtypeStruct((n_seq, SUBLANES, CONV_DIM), F32),
    ]
    out_specs = [
        _rows(tm, GMLP_WIDTH), _rows(tm, GMLP_WIDTH), _rows(tm, SSM_INNER),
        _rows(tm, SSM_INNER), _rows(tm, BC_WIDTH), _rows(tm, BC_WIDTH), _rows(tm, LANES),
        _rows(tm, D_MODEL), _rows(tm, D_MODEL), cstate_spec,
    ]
    if emit_vf:
        out_shape.append(jax.ShapeDtypeStruct((t, GMLP_WIDTH), F32))
        out_specs.append(_rows(tm, GMLP_WIDTH))
    return pl.pallas_call(
        functools.partial(_front_kernel, n_seg, carry, tiles_per_seq),
        grid=(n_tiles,),
        in_specs=in_specs,
        out_specs=out_specs,
        out_shape=out_shape,
        scratch_shapes=[pltpu.VMEM((SUBLANES, CONV_DIM), F32)],
        compiler_params=pltpu.CompilerParams(dimension_semantics=("arbitrary",), vmem_limit_bytes=VMEM_LIMIT),
        name="front_carry" if carry else "front_seg",
    )(x, hist, *wts)


def _mixer_kernel(n_seg, carry, tiles_per_seq,
                  x_ref, u_ref, v_ref, zs_ref, xs_ref, bm_ref, cm_ref, dt_ref, ga_ref, gb_ref, h0_ref,
                  wsp_ref, bsp_ref, w_a_ref, w_b_ref, w_out_ref, nrm_w_ref, post_w_ref, a_ref, dskip_ref,
                  exp_ref,
                  y_ref, hout_ref,
                  state_ref):
    tm = x_ref.shape[0]
    seg_len = tm // n_seg
    gw = SSM_GROUP_WIDTH

    if carry:
        @pl.when((pl.program_id(0) % tiles_per_seq) == 0)
        def _():
            for g in range(SSM_GROUPS):
                state_ref[g] = h0_ref[0, g * gw:(g + 1) * gw, :].T

    v = v_ref[...]
    sp = [_dot(wsp_ref[g], v[:, g * GMLP_GROUP_DIM:(g + 1) * GMLP_GROUP_DIM]) for g in range(GMLP_GROUPS)]
    o_a = u_ref[...].astype(F32) * (jnp.concatenate(sp, axis=1) + bsp_ref[...])
    y_a = _dot(o_a.astype(BF16), w_a_ref[...])

    qd = SSD_BLOCK
    qs = min(qd, seg_len)
    blocks_per_seg = seg_len // qs
    seg_shift = seg_len.bit_length() - 1
    row = lax.broadcasted_iota(jnp.int32, (tm, tm), 0)
    col = lax.broadcasted_iota(jnp.int32, (tm, tm), 1)
    causal = (row >= col) & ((row >> seg_shift) == (col >> seg_shift))
    tri = causal[:qd, :qd]
    dt = dt_ref[...]
    dta = dt * a_ref[...]
    ones_tri = jnp.where(causal, 1.0, 0.0).astype(BF16)
    acum3 = _dot(ones_tri, _split3(dta))
    acum = acum3[:, :LANES] + acum3[:, LANES:2 * LANES] + acum3[:, 2 * LANES:]
    acum2 = acum * LOG2E
    acum2_t = acum2.T
    acum_x = _dot(_split3(acum), exp_ref[...])
    dt_x = _dot(_split3(dt), exp_ref[...])
    xs = xs_ref[...].astype(F32)
    xdt = xs * dt_x
    xdt_b = xdt.astype(BF16)
    e_off, xw = [], []
    for b in range(tm // qs):
        blk = acum_x[b * qs:(b + 1) * qs]
        if b % blocks_per_seg:
            blk = blk - acum_x[b * qs - 1:b * qs]
        e_off.append(jnp.exp(blk))
        xw.append((xdt[b * qs:(b + 1) * qs] * jnp.exp(blk[qs - 1:qs] - blk)).astype(BF16))
    low_half = lax.broadcasted_iota(jnp.int32, (qd, LANES), 1) < SSM_HEAD_DIM

    bm = bm_ref[...]
    cm = cm_ref[...]
    y_groups = []
    for g in range(SSM_GROUPS):
        gcols = slice(g * gw, (g + 1) * gw)
        y_blocks = []
        h_run = None
        for d in range(tm // qd):
            d0, d1 = d * qd, (d + 1) * qd
            cg = cm[d0:d1, g * SSM_STATE:(g + 1) * SSM_STATE]
            bg = bm[d0:d1, g * SSM_STATE:(g + 1) * SSM_STATE]
            cb = jnp.where(tri, _dot_nt(cg, bg), 0.0)
            y_pairs = []
            for j in range(SSM_HEADS_PER_GROUP // 2):
                lhs = []
                for hh in (2 * j, 2 * j + 1):
                    hd = g * SSM_HEADS_PER_GROUP + hh
                    seg = acum2[d0:d1, hd:hd + 1] - acum2_t[hd:hd + 1, d0:d1]
                    lhs.append((jnp.exp2(jnp.minimum(seg, 0.0)) * cb).astype(BF16))
                c0 = g * gw + j * LANES
                pair = xdt_b[d0:d1, c0:c0 + LANES]
                zero = jnp.zeros_like(pair)
                rhs = jnp.concatenate([jnp.where(low_half, pair, zero), jnp.where(low_half, zero, pair)], axis=0)
                y_pairs.append(_dot(jnp.concatenate(lhs, axis=1), rhs))
            y_diag = jnp.concatenate(y_pairs, axis=1)
            y_off = []
            for b in range(d * (qd // qs), (d + 1) * (qd // qs)):
                r0, r1 = b * qs - d0, (b + 1) * qs - d0
                s = b // blocks_per_seg
                if b % blocks_per_seg == 0:
                    h_run = state_ref[g] if carry else h0_ref[s, gcols, :].T
                eo = e_off[b][:, gcols]
                y_off.append(_dot(cg[r0:r1], h_run.astype(BF16)) * eo)
                bt = bg[r0:r1].astype(F32).T.astype(BF16)
                h_run = h_run * eo[qs - 1:qs] + _dot(bt, xw[b][:, gcols])
                if (b + 1) % blocks_per_seg == 0:
                    if carry:
                        state_ref[g] = h_run
                    else:
                        hout_ref[s, gcols, :] = h_run.T
            y_blocks.append(y_diag + (y_off[0] if len(y_off) == 1 else jnp.concatenate(y_off, axis=0)))
        y_groups.append(y_blocks[0] if len(y_blocks) == 1 else jnp.concatenate(y_blocks, axis=0))
    y = jnp.concatenate(y_groups, axis=1) + dskip_ref[...] * xs

    gz = y * zs_ref[...].astype(F32)
    nw = nrm_w_ref[...]
    o_b = []
    for g in range(SSM_GROUPS):
        blk = gz[:, g * gw:(g + 1) * gw]
        blk = blk * lax.rsqrt(jnp.mean(blk * blk, axis=-1, keepdims=True) + EPS)
        o_b.append((blk * nw[:, g * gw:(g + 1) * gw]).astype(BF16))
    y_b = _dot(jnp.concatenate(o_b, axis=1), w_b_ref[...])

    merged = ga_ref[...].astype(F32) * y_a + gb_ref[...].astype(F32) * y_b
    mixed = _dot(merged.astype(BF16), w_out_ref[...])
    y_ref[...] = x_ref[...] + _rms_norm(mixed, post_w_ref[...])

    if carry:
        @pl.when((pl.program_id(0) % tiles_per_seq) == tiles_per_seq - 1)
        def _():
            for g in range(SSM_GROUPS):
                hout_ref[0, g * gw:(g + 1) * gw, :] = state_ref[g].T


def _mixer_call(x, front, h0, wts, *, tm, n_seg, carry, seq_len):
    t = x.shape[0]
    tiles_per_seq = seq_len // tm if carry else 1
    n_tiles = t // tm
    u, v, zs, xs, bm, cm, dt, ga, gb = front
    hp = SSM_HEADS * SSM_HEAD_DIM
    if carry:
        state_spec = pl.BlockSpec((1, hp, SSM_STATE), lambda i: (i // tiles_per_seq, 0, 0))
    else:
        state_spec = pl.BlockSpec((n_seg, hp, SSM_STATE), lambda i: (i, 0, 0))
    in_specs = [
        _rows(tm, D_MODEL), _rows(tm, GMLP_WIDTH), _rows(tm, GMLP_WIDTH), _rows(tm, SSM_INNER), _rows(tm, SSM_INNER),
        _rows(tm, BC_WIDTH), _rows(tm, BC_WIDTH), _rows(tm, LANES), _rows(tm, D_MODEL), _rows(tm, D_MODEL),
        state_spec,
    ] + [_resident(w.shape) for w in wts]
    return pl.pallas_call(
        functools.partial(_mixer_kernel, n_seg, carry, tiles_per_seq),
        grid=(n_tiles,),
        in_specs=in_specs,
        out_specs=(_rows(tm, D_MODEL), state_spec),
        out_shape=(jax.ShapeDtypeStruct((t, D_MODEL), F32), jax.ShapeDtypeStruct(h0.shape, F32)),
        scratch_shapes=[pltpu.VMEM((SSM_GROUPS, SSM_STATE, SSM_GROUP_WIDTH), F32)],
        compiler_params=pltpu.CompilerParams(dimension_semantics=("arbitrary",), vmem_limit_bytes=VMEM_LIMIT),
        name="mixer_carry" if carry else "mixer_seg",
    )(x, u, v, zs, xs, bm, cm, dt, ga, gb, h0, *wts)


def _ffn_kernel(x_ref, pre_w_ref, w_up_ref, w_down_ref, post_w_ref, y_ref):
    x = x_ref[...]
    h = _rms_norm(x, pre_w_ref[...]).astype(BF16)
    acc = None
    for c in range(FFN_HIDDEN // FFN_CHUNK):
        up = _dot(h, w_up_ref[:, c * FFN_CHUNK:(c + 1) * FFN_CHUNK])
        act = jnp.square(jnp.maximum(up, 0.0)).astype(BF16)
        part = _dot(act, w_down_ref[c * FFN_CHUNK:(c + 1) * FFN_CHUNK, :])
        acc = part if acc is None else acc + part
    y_ref[...] = x + _rms_norm(acc, post_w_ref[...])


def _ffn_call(x, wts, *, tm):
    t = x.shape[0]
    return pl.pallas_call(
        _ffn_kernel,
        grid=(t // tm,),
        in_specs=[_rows(tm, D_MODEL)] + [_resident(w.shape) for w in wts],
        out_specs=_rows(tm, D_MODEL),
        out_shape=jax.ShapeDtypeStruct((t, D_MODEL), F32),
        compiler_params=pltpu.CompilerParams(dimension_semantics=("arbitrary",), vmem_limit_bytes=VMEM_LIMIT),
        name="ffn",
    )(x, *wts)


def _spatial_tile(ws, bs, n, tm):
    pos = jnp.arange(n)
    mask = (pos[None, :] // CHUNK) <= (pos[:, None] // CHUNK)
    w = jnp.where(mask[None], ws[:, :n, :n], 0.0)
    reps = tm // n
    eye = jnp.eye(reps, dtype=w.dtype)
    tile = jnp.einsum("ab,gij->gaibj", eye, w).reshape(GMLP_GROUPS, tm, tm)
    bias = jnp.repeat(jnp.tile(bs[:, :n].T, (reps, 1)), GMLP_GROUP_DIM, axis=1)
    return tile.astype(BF16), bias.astype(F32)


def _layer(x, hist, h0, *, seq_len, carry, want_v, lw):
    (pre_mix_w, w_in, gmlp_ln_w, gmlp_ln_b, gmlp_ws, gmlp_bs, conv_w, conv_b, dt_bias, a_log, d_skip, ssm_norm_w,
     w_branch_a, w_branch_b, w_out, post_mix_w, pre_ffn_w, w_up, w_down, post_ffn_w) = lw
    n_seq = x.shape[0]
    t = n_seq * seq_len
    tm = TILE_M if carry else SEG_TILE_M
    n_seg = 1 if carry else tm // seq_len
    xf = x.reshape(t, D_MODEL)

    p_dt = MAIN_COLS
    p_ga = p_dt + SSM_HEADS
    wb = w_in.astype(BF16)
    w_tail = jnp.concatenate([wb[:, p_ga:], wb[:, p_dt:p_ga], jnp.zeros((D_MODEL, LANES - SSM_HEADS), BF16)], axis=1)
    row = lambda a: a.reshape(1, -1).astype(F32)
    front_w = (row(pre_mix_w), wb, w_tail, row(gmlp_ln_w), row(gmlp_ln_b), conv_w.astype(F32), row(conv_b),
               jnp.pad(row(dt_bias), ((0, 0), (0, LANES - SSM_HEADS))))
    hist8 = jnp.pad(hist.astype(F32), ((0, 0), (SUBLANES - (CONV_WIDTH - 1), 0), (0, 0)))
    u, v, zs, xs, bm, cm, dt, ga, gb, cstate, *vf = _front_call(xf, hist8, front_w, tm=tm, n_seg=n_seg, carry=carry,
                                                                seq_len=seq_len, emit_vf=want_v)

    n_pos = min(seq_len, GMLP_CHUNK)
    wsp, bsp = _spatial_tile(gmlp_ws, gmlp_bs, n_pos, tm)
    a_pad = jnp.pad(row(-jnp.exp(a_log.astype(F32))), ((0, 0), (0, LANES - SSM_HEADS)))
    expand = jnp.repeat(jnp.eye(LANES, SSM_HEADS, dtype=F32), SSM_HEAD_DIM, axis=1)
    expand3 = jnp.tile(expand, (3, 1)).astype(BF16)
    dskip_x = jnp.repeat(row(d_skip), SSM_HEAD_DIM, axis=1)
    mixer_w = (wsp, bsp, w_branch_a.astype(BF16), w_branch_b.astype(BF16), w_out.astype(BF16), row(ssm_norm_w),
               row(post_mix_w), a_pad, dskip_x, expand3)
    h0f = h0.astype(F32).reshape(n_seq, SSM_HEADS * SSM_HEAD_DIM, SSM_STATE)
    x1, h_last = _mixer_call(xf, (u, v, zs, xs, bm, cm, dt, ga, gb), h0f, mixer_w, tm=tm, n_seg=n_seg,
                             carry=carry, seq_len=seq_len)

    ffn_w = (row(pre_ffn_w), w_up.astype(BF16), w_down.astype(BF16), row(post_ffn_w))
    y = _ffn_call(x1, ffn_w, tm=FFN_TILE_M)
    return (y.reshape(n_seq, seq_len, D_MODEL), cstate[:, SUBLANES - (CONV_WIDTH - 1):],
            h_last.reshape(n_seq, SSM_HEADS, SSM_HEAD_DIM, SSM_STATE).astype(h0.dtype),
            vf[0].reshape(n_seq, seq_len, GMLP_WIDTH) if want_v else None)


def kernel(x_prompt, x_sample, cache_conv, state_ssm, pre_mix_w, w_in, gmlp_ln_w, gmlp_ln_b, gmlp_ws, gmlp_bs, conv_w, conv_b, dt_bias, a_log, d_skip, ssm_norm_w, w_branch_a, w_branch_b, w_out, post_mix_w, pre_ffn_w, w_up, w_down, post_ffn_w):
    depth = w_in.shape[0]
    yp, ys = x_prompt, x_sample
    n_p = x_prompt.shape[0]
    hist0 = jnp.zeros((n_p, CONV_WIDTH - 1, CONV_DIM), x_prompt.dtype)
    h00 = jnp.zeros((n_p, SSM_HEADS, SSM_HEAD_DIM, SSM_STATE), state_ssm.dtype)
    conv_p, ssm_p, conv_s, ssm_s, v_s = [], [], [], [], []
    for l in range(depth):
        lw = (pre_mix_w[l], w_in[l], gmlp_ln_w[l], gmlp_ln_b[l], gmlp_ws[l], gmlp_bs[l], conv_w[l], conv_b[l],
              dt_bias[l], a_log[l], d_skip[l], ssm_norm_w[l], w_branch_a[l], w_branch_b[l], w_out[l],
              post_mix_w[l], pre_ffn_w[l], w_up[l], w_down[l], post_ffn_w[l])
        yp, cp, sp, _ = _layer(yp, hist0, h00, seq_len=x_prompt.shape[1], carry=True, want_v=False, lw=lw)
        ys, cs, ss, vs = _layer(ys, cache_conv[l], state_ssm[l], seq_len=x_sample.shape[1], carry=False, want_v=True,
                                lw=lw)
        conv_p.append(cp)
        ssm_p.append(sp)
        conv_s.append(cs)
        ssm_s.append(ss)
        v_s.append(vs)
    return (yp, ys, jnp.stack(conv_p), jnp.stack(ssm_p), jnp.stack(conv_s), jnp.stack(ssm_s), jnp.stack(v_s))
```

```python
import functools

import jax
import jax.numpy as jnp
from jax import lax
from jax.experimental import pallas as pl
from jax.experimental.pallas import tpu as pltpu

D_MODEL = 1024
CHUNK = 64
GMLP_CHUNK = 128
GMLP_WIDTH = 1024
GMLP_GROUPS = 8
GMLP_GROUP_DIM = GMLP_WIDTH // GMLP_GROUPS
SSM_INNER = 2 * D_MODEL
SSM_HEAD_DIM = 64
SSM_HEADS = SSM_INNER // SSM_HEAD_DIM
SSM_GROUPS = 4
SSM_HEADS_PER_GROUP = SSM_HEADS // SSM_GROUPS
SSM_STATE = 128
SSM_GROUP_WIDTH = SSM_INNER // SSM_GROUPS
BC_WIDTH = SSM_GROUPS * SSM_STATE
CONV_WIDTH = 4
CONV_DIM = SSM_INNER + 2 * BC_WIDTH
FFN_HIDDEN = 4 * D_MODEL
EPS = 1e-6
LOG2E = 1.4426950408889634

LANES = 128
SUBLANES = 8
TILE_M = 256
SEG_TILE_M = 128
FFN_TILE_M = 512
FFN_CHUNK = 1024
SSD_BLOCK = 128
FRONT_CHUNK = 512
VMEM_LIMIT = 56 * 1024 * 1024

BF16 = jnp.bfloat16
F32 = jnp.float32

COL_U = 0
COL_V = COL_U + GMLP_WIDTH
COL_Z = COL_V + GMLP_WIDTH
COL_XBC = COL_Z + SSM_INNER
MAIN_COLS = COL_XBC + CONV_DIM
TAIL_GA = 0
TAIL_GB = TAIL_GA + D_MODEL
TAIL_DT = TAIL_GB + D_MODEL


def _dot(a, b):
    return jnp.dot(a, b, preferred_element_type=F32)


def _dot_nt(a, b):
    return lax.dot_general(a, b, (((1,), (1,)), ((), ())), preferred_element_type=F32)


def _rms_norm(x, w):
    return x * lax.rsqrt(jnp.mean(x * x, axis=-1, keepdims=True) + EPS) * w


def _sigmoid(x):
    return 1.0 / (1.0 + jnp.exp2(x * (-LOG2E)))


def _silu(x):
    return x * _sigmoid(x)


def _gelu_tanh(x):
    c = 0.7978845608028654
    return 0.5 * x * (1.0 + jnp.tanh(c * (x + 0.044715 * (x * x * x))))


def _softplus(x):
    return jnp.maximum(x, 0.0) + jnp.log(1.0 + jnp.exp(-jnp.abs(x)))


def _split3(x):
    hi = x.astype(BF16)
    r1 = x - hi.astype(F32)
    mid = r1.astype(BF16)
    lo = (r1 - mid.astype(F32)).astype(BF16)
    return jnp.concatenate([hi, mid, lo], axis=1)


def _resident(shape):
    nd = len(shape)
    return pl.BlockSpec(shape, lambda i: (0,) * nd, pipeline_mode=pl.Buffered(1))


def _rows(tm, width):
    return pl.BlockSpec((tm, width), lambda i: (i, 0))


def _front_kernel(n_seg, carry, tiles_per_seq,
                  x_ref, hist_ref, pre_w_ref, w_ref, w_tail_ref, ln_w_ref, ln_b_ref, conv_w_ref, conv_b_ref, dt_bias_ref,
                  u_ref, v_ref, zs_ref, xs_ref, bm_ref, cm_ref, dt_ref, ga_ref, gb_ref, cstate_ref, *rest):
    tail_ref = rest[-1]
    vf_ref = rest[0] if len(rest) == 2 else None
    tm = x_ref.shape[0]
    seg_len = tm // n_seg
    cc = FRONT_CHUNK
    x = x_ref[...]
    h = _rms_norm(x, pre_w_ref[...]).astype(BF16)
    if carry:
        first = (pl.program_id(0) % tiles_per_seq) == 0

    def conv_chunk(c):
        cols = slice(c * cc, (c + 1) * cc)
        raw = _dot(h, w_ref[:, COL_XBC + c * cc:COL_XBC + (c + 1) * cc])
        cw = conv_w_ref[:, cols]
        outs = []
        for s in range(n_seg):
            seg = raw[s * seg_len:(s + 1) * seg_len]
            if carry:
                prev = jnp.where(first, hist_ref[0, :, cols], tail_ref[:, cols])
            else:
                prev = hist_ref[s, :, cols]
            xp = jnp.concatenate([prev, seg], axis=0)
            acc = conv_b_ref[:, cols] + seg * cw[CONV_WIDTH - 1:CONV_WIDTH]
            for k in range(1, CONV_WIDTH):
                shifted = pltpu.roll(xp, k, 0)[SUBLANES:]
                acc = acc + shifted * cw[CONV_WIDTH - 1 - k:CONV_WIDTH - k]
            outs.append(_silu(acc))
            cstate_ref[s, :, cols] = seg[seg_len - SUBLANES:]
        if carry:
            tail_ref[:, cols] = raw[tm - SUBLANES:]
        act = (outs[0] if n_seg == 1 else jnp.concatenate(outs, axis=0)).astype(BF16)
        if c < SSM_INNER // cc:
            xs_ref[:, cols] = act
        elif c < (SSM_INNER + BC_WIDTH) // cc:
            bm_ref[:, c * cc - SSM_INNER:(c + 1) * cc - SSM_INNER] = act
        else:
            cm_ref[:, c * cc - SSM_INNER - BC_WIDTH:(c + 1) * cc - SSM_INNER - BC_WIDTH] = act

    def proj(col0, c, w=w_ref):
        return _dot(h, w[:, col0 + c * cc:col0 + (c + 1) * cc])

    for c in range(CONV_DIM // cc):
        conv_chunk(c)
    for c in range(D_MODEL // cc):
        ga_ref[:, c * cc:(c + 1) * cc] = _sigmoid(proj(TAIL_GA, c, w_tail_ref)).astype(BF16)
        gb_ref[:, c * cc:(c + 1) * cc] = _sigmoid(proj(TAIL_GB, c, w_tail_ref)).astype(BF16)
    for c in range(SSM_INNER // cc):
        zs_ref[:, c * cc:(c + 1) * cc] = _silu(proj(COL_Z, c)).astype(BF16)
    for c in range(GMLP_WIDTH // cc):
        u_ref[:, c * cc:(c + 1) * cc] = _gelu_tanh(proj(COL_U, c)).astype(BF16)

    v = jnp.concatenate([_gelu_tanh(proj(COL_V, c)) for c in range(GMLP_WIDTH // cc)], axis=1)
    mu = jnp.mean(v, axis=-1, keepdims=True)
    vc = v - mu
    var = jnp.mean(vc * vc, axis=-1, keepdims=True)
    vn = vc * lax.rsqrt(var + EPS) * ln_w_ref[...] + ln_b_ref[...]
    v_ref[...] = vn.astype(BF16)
    if vf_ref is not None:
        vf_ref[...] = vn
    dt_ref[...] = _softplus(_dot(h, w_tail_ref[:, TAIL_DT:TAIL_DT + LANES]) + dt_bias_ref[...])


def _front_call(x, hist, wts, *, tm, n_seg, carry, seq_len, emit_vf):
    t = x.shape[0]
    tiles_per_seq = seq_len // tm if carry else 1
    n_tiles = t // tm
    if carry:
        hist_spec = pl.BlockSpec((1, SUBLANES, CONV_DIM), lambda i: (i // tiles_per_seq, 0, 0))
        cstate_spec = pl.BlockSpec((1, SUBLANES, CONV_DIM), lambda i: (i // tiles_per_seq, 0, 0))
    else:
        hist_spec = pl.BlockSpec((n_seg, SUBLANES, CONV_DIM), lambda i: (i, 0, 0))
        cstate_spec = pl.BlockSpec((n_seg, SUBLANES, CONV_DIM), lambda i: (i, 0, 0))
    n_seq = hist.shape[0]
    w_specs = [_resident(w.shape) for w in wts]
    w_specs[1] = _resident((D_MODEL, MAIN_COLS + LANES))
    in_specs = [_rows(tm, D_MODEL), hist_spec] + w_specs
    out_shape = [
        jax.ShapeDtypeStruct((t, GMLP_WIDTH), BF16),
        jax.ShapeDtypeStruct((t, GMLP_WIDTH), BF16),
        jax.ShapeDtypeStruct((t, SSM_INNER), BF16),
        jax.ShapeDtypeStruct((t, SSM_INNER), BF16),
        jax.ShapeDtypeStruct((t, BC_WIDTH), BF16),
        jax.ShapeDtypeStruct((t, BC_WIDTH), BF16),
        jax.ShapeDtypeStruct((t, LANES), F32),
        jax.ShapeDtypeStruct((t, D_MODEL), BF16),
        jax.ShapeDtypeStruct((t, D_MODEL), BF16),
        jax.ShapeDtypeStruct((n_seq, SUBLANES, CONV_DIM), F32),
    ]
    out_specs = [
        _rows(tm, GMLP_WIDTH), _rows(tm, GMLP_WIDTH), _rows(tm, SSM_INNER),
        _rows(tm, SSM_INNER), _rows(tm, BC_WIDTH), _rows(tm, BC_WIDTH), _rows(tm, LANES),
        _rows(tm, D_MODEL), _rows(tm, D_MODEL), cstate_spec,
    ]
    if emit_vf:
        out_shape.append(jax.ShapeDtypeStruct((t, GMLP_WIDTH), F32))
        out_specs.append(_rows(tm, GMLP_WIDTH))
    return pl.pallas_call(
        functools.partial(_front_kernel, n_seg, carry, tiles_per_seq),
        grid=(n_tiles,),
        in_specs=in_specs,
        out_specs=out_specs,
        out_shape=out_shape,
        scratch_shapes=[pltpu.VMEM((SUBLANES, CONV_DIM), F32)],
        compiler_params=pltpu.CompilerParams(dimension_semantics=("arbitrary",), vmem_limit_bytes=VMEM_LIMIT),
        name="front_carry" if carry else "front_seg",
    )(x, hist, *wts)


def _mixer_kernel(n_seg, carry, tiles_per_seq,
                  x_ref, u_ref, v_ref, zs_ref, xs_ref, bm_ref, cm_ref, dt_ref, ga_ref, gb_ref, h0_ref,
                  wsp_ref, bsp_ref, w_a_ref, w_b_ref, w_out_ref, nrm_w_ref, post_w_ref, a_ref, dskip_ref,
                  exp_ref,
                  y_ref, hout_ref,
                  state_ref):
    tm = x_ref.shape[0]
    seg_len = tm // n_seg
    gw = SSM_GROUP_WIDTH

    if carry:
        @pl.when((pl.program_id(0) % tiles_per_seq) == 0)
        def _():
            for g in range(SSM_GROUPS):
                state_ref[g] = h0_ref[0, g * gw:(g + 1) * gw, :].T

    v = v_ref[...]
    sp = [_dot(wsp_ref[g], v[:, g * GMLP_GROUP_DIM:(g + 1) * GMLP_GROUP_DIM]) for g in range(GMLP_GROUPS)]
    o_a = u_ref[...].astype(F32) * (jnp.concatenate(sp, axis=1) + bsp_ref[...])
    y_a = _dot(o_a.astype(BF16), w_a_ref[...])

    qd = SSD_BLOCK
    qs = min(qd, seg_len)
    blocks_per_seg = seg_len // qs
    seg_shift = seg_len.bit_length() - 1
    row = lax.broadcasted_iota(jnp.int32, (tm, tm), 0)
    col = lax.broadcasted_iota(jnp.int32, (tm, tm), 1)
    causal = (row >= col) & ((row >> seg_shift) == (col >> seg_shift))
    tri = causal[:qd, :qd]
    dt = dt_ref[...]
    dta = dt * a_ref[...]
    ones_tri = jnp.where(causal, 1.0, 0.0).astype(BF16)
    acum3 = _dot(ones_tri, _split3(dta))
    acum = acum3[:, :LANES] + acum3[:, LANES:2 * LANES] + acum3[:, 2 * LANES:]
    acum2 = acum * LOG2E
    acum2_t = acum2.T
    acum_x = _dot(_split3(acum), exp_ref[...])
    dt_x = _dot(_split3(dt), exp_ref[...])
    xs = xs_ref[...].astype(F32)
    xdt = xs * dt_x
    xdt_b = xdt.astype(BF16)
    e_off, xw = [], []
    for b in range(tm // qs):
        blk = acum_x[b * qs:(b + 1) * qs]
        if b % blocks_per_seg:
            blk = blk - acum_x[b * qs - 1:b * qs]
        e_off.append(jnp.exp(blk))
        xw.append((xdt[b * qs:(b + 1) * qs] * jnp.exp(blk[qs - 1:qs] - blk)).astype(BF16))
    low_half = lax.broadcasted_iota(jnp.int32, (qd, LANES), 1) < SSM_HEAD_DIM

    bm = bm_ref[...]
    cm = cm_ref[...]
    y_groups = []
    for g in range(SSM_GROUPS):
        gcols = slice(g * gw, (g + 1) * gw)
        y_blocks = []
        h_run = None
        for d in range(tm // qd):
            d0, d1 = d * qd, (d + 1) * qd
            cg = cm[d0:d1, g * SSM_STATE:(g + 1) * SSM_STATE]
            bg = bm[d0:d1, g * SSM_STATE:(g + 1) * SSM_STATE]
            cb = jnp.where(tri, _dot_nt(cg, bg), 0.0)
            y_pairs = []
            for j in range(SSM_HEADS_PER_GROUP // 2):
                lhs = []
                for hh in (2 * j, 2 * j + 1):
                    hd = g * SSM_HEADS_PER_GROUP + hh
                    seg = acum2[d0:d1, hd:hd + 1] - acum2_t[hd:hd + 1, d0:d1]
                    lhs.append((jnp.exp2(jnp.minimum(seg, 0.0)) * cb).astype(BF16))
                c0 = g * gw + j * LANES
                pair = xdt_b[d0:d1, c0:c0 + LANES]
                zero = jnp.zeros_like(pair)
                rhs = jnp.concatenate([jnp.where(low_half, pair, zero), jnp.where(low_half, zero, pair)], axis=0)
                y_pairs.append(_dot(jnp.concatenate(lhs, axis=1), rhs))
            y_diag = jnp.concatenate(y_pairs, axis=1)
            y_off = []
            for b in range(d * (qd // qs), (d + 1) * (qd // qs)):
                r0, r1 = b * qs - d0, (b + 1) * qs - d0
                s = b // blocks_per_seg
                if b % blocks_per_seg == 0:
                    h_run = state_ref[g] if carry else h0_ref[s, gcols, :].T
                eo = e_off[b][:, gcols]
                y_off.append(_dot(cg[r0:r1], h_run.astype(BF16)) * eo)
                bt = bg[r0:r1].astype(F32).T.astype(BF16)
                h_run = h_run * eo[qs - 1:qs] + _dot(bt, xw[b][:, gcols])
                if (b + 1) % blocks_per_seg == 0:
                    if carry:
                        state_ref[g] = h_run
                    else:
                        hout_ref[s, gcols, :] = h_run.T
            y_blocks.append(y_diag + (y_off[0] if len(y_off) == 1 else jnp.concatenate(y_off, axis=0)))
        y_groups.append(y_blocks[0] if len(y_blocks) == 1 else jnp.concatenate(y_blocks, axis=0))
    y = jnp.concatenate(y_groups, axis=1) + dskip_ref[...] * xs

    gz = y * zs_ref[...].astype(F32)
    nw = nrm_w_ref[...]
    o_b = []
    for g in range(SSM_GROUPS):
        blk = gz[:, g * gw:(g + 1) * gw]
        blk = blk * lax.rsqrt(jnp.mean(blk * blk, axis=-1, keepdims=True) + EPS)
        o_b.append((blk * nw[:, g * gw:(g + 1) * gw]).astype(BF16))
    y_b = _dot(jnp.concatenate(o_b, axis=1), w_b_ref[...])

    merged = ga_ref[...].astype(F32) * y_a + gb_ref[...].astype(F32) * y_b
    mixed = _dot(merged.astype(BF16), w_out_ref[...])
    y_ref[...] = x_ref[...] + _rms_norm(mixed, post_w_ref[...])

    if carry:
        @pl.when((pl.program_id(0) % tiles_per_seq) == tiles_per_seq - 1)
        def _():
            for g in range(SSM_GROUPS):
                hout_ref[0, g * gw:(g + 1) * gw, :] = state_ref[g].T


def _mixer_call(x, front, h0, wts, *, tm, n_seg, carry, seq_len):
    t = x.shape[0]
    tiles_per_seq = seq_len // tm if carry else 1
    n_tiles = t // tm
    u, v, zs, xs, bm, cm, dt, ga, gb = front
    hp = SSM_HEADS * SSM_HEAD_DIM
    if carry:
        state_spec = pl.BlockSpec((1, hp, SSM_STATE), lambda i: (i // tiles_per_seq, 0, 0))
    else:
        state_spec = pl.BlockSpec((n_seg, hp, SSM_STATE), lambda i: (i, 0, 0))
    in_specs = [
        _rows(tm, D_MODEL), _rows(tm, GMLP_WIDTH), _rows(tm, GMLP_WIDTH), _rows(tm, SSM_INNER), _rows(tm, SSM_INNER),
        _rows(tm, BC_WIDTH), _rows(tm, BC_WIDTH), _rows(tm, LANES), _rows(tm, D_MODEL), _rows(tm, D_MODEL),
        state_spec,
    ] + [_resident(w.shape) for w in wts]
    return pl.pallas_call(
        functools.partial(_mixer_kernel, n_seg, carry, tiles_per_seq),
        grid=(n_tiles,),
        in_specs=in_specs,
        out_specs=(_rows(tm, D_MODEL), state_spec),
        out_shape=(jax.ShapeDtypeStruct((t, D_MODEL), F32), jax.ShapeDtypeStruct(h0.shape, F32)),
        scratch_shapes=[pltpu.VMEM((SSM_GROUPS, SSM_STATE, SSM_GROUP_WIDTH), F32)],
        compiler_params=pltpu.CompilerParams(dimension_semantics=("arbitrary",), vmem_limit_bytes=VMEM_LIMIT),
        name="mixer_carry" if carry else "mixer_seg",
    )(x, u, v, zs, xs, bm, cm, dt, ga, gb, h0, *wts)


def _ffn_kernel(x_ref, pre_w_ref, w_up_ref, w_down_ref, post_w_ref, y_ref):
    x = x_ref[...]
    h = _rms_norm(x, pre_w_ref[...]).astype(BF16)
    acc = None
    for c in range(FFN_HIDDEN // FFN_CHUNK):
        up = _dot(h, w_up_ref[:, c * FFN_CHUNK:(c + 1) * FFN_CHUNK])
        act = jnp.square(jnp.maximum(up, 0.0)).astype(BF16)
        part = _dot(act, w_down_ref[c * FFN_CHUNK:(c + 1) * FFN_CHUNK, :])
        acc = part if acc is None else acc + part
    y_ref[...] = x + _rms_norm(acc, post_w_ref[...])


def _ffn_call(x, wts, *, tm):
    t = x.shape[0]
    return pl.pallas_call(
        _ffn_kernel,
        grid=(t // tm,),
        in_specs=[_rows(tm, D_MODEL)] + [_resident(w.shape) for w in wts],
        out_specs=_rows(tm, D_MODEL),
        out_shape=jax.ShapeDtypeStruct((t, D_MODEL), F32),
        compiler_params=pltpu.CompilerParams(dimension_semantics=("arbitrary",), vmem_limit_bytes=VMEM_LIMIT),
        name="ffn",
    )(x, *wts)


def _spatial_tile(ws, bs, n, tm):
    pos = jnp.arange(n)
    mask = (pos[None, :] // CHUNK) <= (pos[:, None] // CHUNK)
    w = jnp.where(mask[None], ws[:, :n, :n], 0.0)
    reps = tm // n
    eye = jnp.eye(reps, dtype=w.dtype)
    tile = jnp.einsum("ab,gij->gaibj", eye, w).reshape(GMLP_GROUPS, tm, tm)
    bias = jnp.repeat(jnp.tile(bs[:, :n].T, (reps, 1)), GMLP_GROUP_DIM, axis=1)
    return tile.astype(BF16), bias.astype(F32)


def _layer(x, hist, h0, *, seq_len, carry, want_v, lw):
    (pre_mix_w, w_in, gmlp_ln_w, gmlp_ln_b, gmlp_ws, gmlp_bs, conv_w, conv_b, dt_bias, a_log, d_skip, ssm_norm_w,
     w_branch_a, w_branch_b, w_out, post_mix_w, pre_ffn_w, w_up, w_down, post_ffn_w) = lw
    n_seq = x.shape[0]
    t = n_seq * seq_len
    tm = TILE_M if carry else SEG_TILE_M
    n_seg = 1 if carry else tm // seq_len
    xf = x.reshape(t, D_MODEL)

    p_dt = MAIN_COLS
    p_ga = p_dt + SSM_HEADS
    wb = w_in.astype(BF16)
    w_tail = jnp.concatenate([wb[:, p_ga:], wb[:, p_dt:p_ga], jnp.zeros((D_MODEL, LANES - SSM_HEADS), BF16)], axis=1)
    row = lambda a: a.reshape(1, -1).astype(F32)
    front_w = (row(pre_mix_w), wb, w_tail, row(gmlp_ln_w), row(gmlp_ln_b), conv_w.astype(F32), row(conv_b),
               jnp.pad(row(dt_bias), ((0, 0), (0, LANES - SSM_HEADS))))
    hist8 = jnp.pad(hist.astype(F32), ((0, 0), (SUBLANES - (CONV_WIDTH - 1), 0), (0, 0)))
    u, v, zs, xs, bm, cm, dt, ga, gb, cstate, *vf = _front_call(xf, hist8, front_w, tm=tm, n_seg=n_seg, carry=carry,
                                                                seq_len=seq_len, emit_vf=want_v)

    n_pos = min(seq_len, GMLP_CHUNK)
    wsp, bsp = _spatial_tile(gmlp_ws, gmlp_bs, n_pos, tm)
    a_pad = jnp.pad(row(-jnp.exp(a_log.astype(F32))), ((0, 0), (0, LANES - SSM_HEADS)))
    expand = jnp.repeat(jnp.eye(LANES, SSM_HEADS, dtype=F32), SSM_HEAD_DIM, axis=1)
    expand3 = jnp.tile(expand, (3, 1)).astype(BF16)
    dskip_x = jnp.repeat(row(d_skip), SSM_HEAD_DIM, axis=1)
    mixer_w = (wsp, bsp, w_branch_a.astype(BF16), w_branch_b.astype(BF16), w_out.astype(BF16), row(ssm_norm_w),
               row(post_mix_w), a_pad, dskip_x, expand3)
    h0f = h0.astype(F32).reshape(n_seq, SSM_HEADS * SSM_HEAD_DIM, SSM_STATE)
    x1, h_last = _mixer_call(xf, (u, v, zs, xs, bm, cm, dt, ga, gb), h0f, mixer_w, tm=tm, n_seg=n_seg,
                             carry=carry, seq_len=seq_len)

    ffn_w = (row(pre_ffn_w), w_up.astype(BF16), w_down.astype(BF16), row(post_ffn_w))
    y = _ffn_call(x1, ffn_w, tm=FFN_TILE_M)
    return (y.reshape(n_seq, seq_len, D_MODEL), cstate[:, SUBLANES - (CONV_WIDTH - 1):],
            h_last.reshape(n_seq, SSM_HEADS, SSM_HEAD_DIM, SSM_STATE).astype(h0.dtype),
            vf[0].reshape(n_seq, seq_len, GMLP_WIDTH) if want_v else None)


def kernel(x_prompt, x_sample, cache_conv, state_ssm, pre_mix_w, w_in, gmlp_ln_w, gmlp_ln_b, gmlp_ws, gmlp_bs, conv_w, conv_b, dt_bias, a_log, d_skip, ssm_norm_w, w_branch_a, w_branch_b, w_out, post_mix_w, pre_ffn_w, w_up, w_down, post_ffn_w):
    depth = w_in.shape[0]
    yp, ys = x_prompt, x_sample
    n_p = x_prompt.shape[0]
    hist0 = jnp.zeros((n_p, CONV_WIDTH - 1, CONV_DIM), x_prompt.dtype)
    h00 = jnp.zeros((n_p, SSM_HEADS, SSM_HEAD_DIM, SSM_STATE), state_ssm.dtype)
    conv_p, ssm_p, conv_s, ssm_s, v_s = [], [], [], [], []
    for l in range(depth):
        lw = (pre_mix_w[l], w_in[l], gmlp_ln_w[l], gmlp_ln_b[l], gmlp_ws[l], gmlp_bs[l], conv_w[l], conv_b[l],
              dt_bias[l], a_log[l], d_skip[l], ssm_norm_w[l], w_branch_a[l], w_branch_b[l], w_out[l],
              post_mix_w[l], pre_ffn_w[l], w_up[l], w_down[l], post_ffn_w[l])
        yp, cp, sp, _ = _layer(yp, hist0, h00, seq_len=x_prompt.shape[1], carry=True, want_v=False, lw=lw)
        ys, cs, ss, vs = _layer(ys, cache_conv[l], state_ssm[l], seq_len=x_sample.shape[1], carry=False, want_v=True,
                                lw=lw)
        conv_p.append(cp)
        ssm_p.append(sp)
        conv_s.append(cs)
        ssm_s.append(ss)
        v_s.append(vs)
    return (yp, ys, jnp.stack(conv_p), jnp.stack(ssm_p), jnp.stack(conv_s), jnp.stack(ssm_s), jnp.stack(v_s))
```

```python
import functools

import jax
import jax.numpy as jnp
from jax import lax
from jax.experimental import pallas as pl
from jax.experimental.pallas import tpu as pltpu

D_MODEL = 1024
CHUNK = 64
GMLP_CHUNK = 128
GMLP_WIDTH = 1024
GMLP_GROUPS = 8
GMLP_GROUP_DIM = GMLP_WIDTH // GMLP_GROUPS
SSM_INNER = 2 * D_MODEL
SSM_HEAD_DIM = 64
SSM_HEADS = SSM_INNER // SSM_HEAD_DIM
SSM_GROUPS = 4
SSM_HEADS_PER_GROUP = SSM_HEADS // SSM_GROUPS
SSM_STATE = 128
SSM_GROUP_WIDTH = SSM_INNER // SSM_GROUPS
BC_WIDTH = SSM_GROUPS * SSM_STATE
CONV_WIDTH = 4
CONV_DIM = SSM_INNER + 2 * BC_WIDTH
FFN_HIDDEN = 4 * D_MODEL
EPS = 1e-6
LOG2E = 1.4426950408889634

LANES = 128
SUBLANES = 8
TILE_M = 256
SEG_TILE_M = 128
FFN_TILE_M = 512
FFN_CHUNK = 1024
SSD_BLOCK = 128
PROJ_CHUNK = 512
VMEM_LIMIT = 60 * 1024 * 1024

BF16 = jnp.bfloat16
F32 = jnp.float32

COL_U = 0
COL_V = COL_U + GMLP_WIDTH
COL_Z = COL_V + GMLP_WIDTH
COL_XBC = COL_Z + SSM_INNER
MAIN_COLS = COL_XBC + CONV_DIM
TAIL_GA = 0
TAIL_GB = TAIL_GA + D_MODEL
TAIL_DT = TAIL_GB + D_MODEL


def _dot(a, b):
    return jnp.dot(a, b, preferred_element_type=F32)


def _dot_nt(a, b):
    return lax.dot_general(a, b, (((1,), (1,)), ((), ())), preferred_element_type=F32)


def _rms_norm(x, w):
    return x * lax.rsqrt(jnp.mean(x * x, axis=-1, keepdims=True) + EPS) * w


def _sigmoid(x):
    return 1.0 / (1.0 + jnp.exp2(x * (-LOG2E)))


def _silu(x):
    return x * _sigmoid(x)


def _gelu_tanh(x):
    c = 0.7978845608028654
    return 0.5 * x * (1.0 + jnp.tanh(c * (x + 0.044715 * (x * x * x))))


def _softplus(x):
    return jnp.maximum(x, 0.0) + jnp.log(1.0 + jnp.exp(-jnp.abs(x)))


def _split3(x):
    hi = x.astype(BF16)
    r1 = x - hi.astype(F32)
    mid = r1.astype(BF16)
    lo = (r1 - mid.astype(F32)).astype(BF16)
    return jnp.concatenate([hi, mid, lo], axis=1)


def _resident(shape):
    nd = len(shape)
    return pl.BlockSpec(shape, lambda i: (0,) * nd, pipeline_mode=pl.Buffered(1))


def _rows(tm, width):
    return pl.BlockSpec((tm, width), lambda i: (i, 0))


def _mix_kernel(n_seg, carry, tiles_per_seq,
                x_ref, hist_ref, h0_ref, pre_w_ref, w_ref, w_tail_ref, ln_w_ref, ln_b_ref, conv_w_ref, conv_b_ref,
                dt_bias_ref, wsp_ref, bsp_ref, w_a_ref, w_b_ref, w_out_ref, nrm_w_ref, post_w_ref, a_ref, dskip_ref,
                exp_ref,
                y_ref, cstate_ref, hout_ref, *rest):
    tail_ref, state_ref = rest[-2:]
    vf_ref = rest[0] if len(rest) == 3 else None
    tm = x_ref.shape[0]
    seg_len = tm // n_seg
    gw = SSM_GROUP_WIDTH
    cc = PROJ_CHUNK

    if carry:
        first = (pl.program_id(0) % tiles_per_seq) == 0

        @pl.when(first)
        def _():
            for g in range(SSM_GROUPS):
                state_ref[g] = h0_ref[0, g * gw:(g + 1) * gw, :].T

    x = x_ref[...]
    h = _rms_norm(x, pre_w_ref[...]).astype(BF16)

    def proj(col0, c, w=w_ref):
        return _dot(h, w[:, col0 + c * cc:col0 + (c + 1) * cc])

    def conv_chunk(c):
        cols = slice(c * cc, (c + 1) * cc)
        raw = proj(COL_XBC, c)
        cw = conv_w_ref[:, cols]
        outs = []
        for s in range(n_seg):
            seg = raw[s * seg_len:(s + 1) * seg_len]
            if carry:
                prev = jnp.where(first, hist_ref[0, :, cols], tail_ref[:, cols])
            else:
                prev = hist_ref[s, :, cols]
            xp = jnp.concatenate([prev, seg], axis=0)
            acc = conv_b_ref[:, cols] + seg * cw[CONV_WIDTH - 1:CONV_WIDTH]
            for k in range(1, CONV_WIDTH):
                shifted = pltpu.roll(xp, k, 0)[SUBLANES:]
                acc = acc + shifted * cw[CONV_WIDTH - 1 - k:CONV_WIDTH - k]
            outs.append(_silu(acc))
            cstate_ref[s, :, cols] = seg[seg_len - SUBLANES:]
        if carry:
            tail_ref[:, cols] = raw[tm - SUBLANES:]
        return outs[0] if n_seg == 1 else jnp.concatenate(outs, axis=0)

    xbc = jnp.concatenate([conv_chunk(c) for c in range(CONV_DIM // cc)], axis=1)
    xs = xbc[:, :SSM_INNER]
    bm = xbc[:, SSM_INNER:SSM_INNER + BC_WIDTH].astype(BF16)
    cm = xbc[:, SSM_INNER + BC_WIDTH:].astype(BF16)
    ga = jnp.concatenate([_sigmoid(proj(TAIL_GA, c, w_tail_ref)) for c in range(D_MODEL // cc)], axis=1)
    gb = jnp.concatenate([_sigmoid(proj(TAIL_GB, c, w_tail_ref)) for c in range(D_MODEL // cc)], axis=1)
    zs = jnp.concatenate([_silu(proj(COL_Z, c)) for c in range(SSM_INNER // cc)], axis=1)
    u = jnp.concatenate([_gelu_tanh(proj(COL_U, c)) for c in range(GMLP_WIDTH // cc)], axis=1)
    v = jnp.concatenate([_gelu_tanh(proj(COL_V, c)) for c in range(GMLP_WIDTH // cc)], axis=1)
    mu = jnp.mean(v, axis=-1, keepdims=True)
    vc = v - mu
    var = jnp.mean(vc * vc, axis=-1, keepdims=True)
    vn = vc * lax.rsqrt(var + EPS) * ln_w_ref[...] + ln_b_ref[...]
    if vf_ref is not None:
        vf_ref[...] = vn
    dt = _softplus(_dot(h, w_tail_ref[:, TAIL_DT:TAIL_DT + LANES]) + dt_bias_ref[...])

    vb = vn.astype(BF16)
    sp = [_dot(wsp_ref[g], vb[:, g * GMLP_GROUP_DIM:(g + 1) * GMLP_GROUP_DIM]) for g in range(GMLP_GROUPS)]
    o_a = u * (jnp.concatenate(sp, axis=1) + bsp_ref[...])
    y_a = _dot(o_a.astype(BF16), w_a_ref[...])

    qd = SSD_BLOCK
    qs = min(qd, seg_len)
    blocks_per_seg = seg_len // qs
    seg_shift = seg_len.bit_length() - 1
    row = lax.broadcasted_iota(jnp.int32, (tm, tm), 0)
    col = lax.broadcasted_iota(jnp.int32, (tm, tm), 1)
    causal = (row >= col) & ((row >> seg_shift) == (col >> seg_shift))
    tri = causal[:qd, :qd]
    dta = dt * a_ref[...]
    ones_tri = jnp.where(causal, 1.0, 0.0).astype(BF16)
    acum3 = _dot(ones_tri, _split3(dta))
    acum = acum3[:, :LANES] + acum3[:, LANES:2 * LANES] + acum3[:, 2 * LANES:]
    acum2 = acum * LOG2E
    acum2_t = acum2.T
    acum_x = _dot(_split3(acum), exp_ref[...])
    dt_x = _dot(_split3(dt), exp_ref[...])
    xdt = xs * dt_x
    xdt_b = xdt.astype(BF16)
    e_off, xw = [], []
    for b in range(tm // qs):
        blk = acum_x[b * qs:(b + 1) * qs]
        if b % blocks_per_seg:
            blk = blk - acum_x[b * qs - 1:b * qs]
        e_off.append(jnp.exp(blk))
        xw.append((xdt[b * qs:(b + 1) * qs] * jnp.exp(blk[qs - 1:qs] - blk)).astype(BF16))
    low_half = lax.broadcasted_iota(jnp.int32, (qd, LANES), 1) < SSM_HEAD_DIM

    y_groups = []
    for g in range(SSM_GROUPS):
        gcols = slice(g * gw, (g + 1) * gw)
        y_blocks = []
        h_run = None
        for d in range(tm // qd):
            d0, d1 = d * qd, (d + 1) * qd
            cg = cm[d0:d1, g * SSM_STATE:(g + 1) * SSM_STATE]
            bg = bm[d0:d1, g * SSM_STATE:(g + 1) * SSM_STATE]
            cb = jnp.where(tri, _dot_nt(cg, bg), 0.0)
            y_pairs = []
            for j in range(SSM_HEADS_PER_GROUP // 2):
                lhs = []
                for hh in (2 * j, 2 * j + 1):
                    hd = g * SSM_HEADS_PER_GROUP + hh
                    seg = acum2[d0:d1, hd:hd + 1] - acum2_t[hd:hd + 1, d0:d1]
                    lhs.append((jnp.exp2(jnp.minimum(seg, 0.0)) * cb).astype(BF16))
                c0 = g * gw + j * LANES
                pair = xdt_b[d0:d1, c0:c0 + LANES]
                zero = jnp.zeros_like(pair)
                rhs = jnp.concatenate([jnp.where(low_half, pair, zero), jnp.where(low_half, zero, pair)], axis=0)
                y_pairs.append(_dot(jnp.concatenate(lhs, axis=1), rhs))
            y_diag = jnp.concatenate(y_pairs, axis=1)
            y_off = []
            for b in range(d * (qd // qs), (d + 1) * (qd // qs)):
                r0, r1 = b * qs - d0, (b + 1) * qs - d0
                s = b // blocks_per_seg
                if b % blocks_per_seg == 0:
                    h_run = state_ref[g] if carry else h0_ref[s, gcols, :].T
                eo = e_off[b][:, gcols]
                y_off.append(_dot(cg[r0:r1], h_run.astype(BF16)) * eo)
                bt = bg[r0:r1].astype(F32).T.astype(BF16)
                h_run = h_run * eo[qs - 1:qs] + _dot(bt, xw[b][:, gcols])
                if (b + 1) % blocks_per_seg == 0:
                    if carry:
                        state_ref[g] = h_run
                    else:
                        hout_ref[s, gcols, :] = h_run.T
            y_blocks.append(y_diag + (y_off[0] if len(y_off) == 1 else jnp.concatenate(y_off, axis=0)))
        y_groups.append(y_blocks[0] if len(y_blocks) == 1 else jnp.concatenate(y_blocks, axis=0))
    y = jnp.concatenate(y_groups, axis=1) + dskip_ref[...] * xs

    gz = y * zs
    nw = nrm_w_ref[...]
    o_b = []
    for g in range(SSM_GROUPS):
        blk = gz[:, g * gw:(g + 1) * gw]
        blk = blk * lax.rsqrt(jnp.mean(blk * blk, axis=-1, keepdims=True) + EPS)
        o_b.append((blk * nw[:, g * gw:(g + 1) * gw]).astype(BF16))
    y_b = _dot(jnp.concatenate(o_b, axis=1), w_b_ref[...])

    merged = ga * y_a + gb * y_b
    mixed = _dot(merged.astype(BF16), w_out_ref[...])
    y_ref[...] = x + _rms_norm(mixed, post_w_ref[...])

    if carry:
        @pl.when((pl.program_id(0) % tiles_per_seq) == tiles_per_seq - 1)
        def _():
            for g in range(SSM_GROUPS):
                hout_ref[0, g * gw:(g + 1) * gw, :] = state_ref[g].T


def _mix_call(x, hist, h0, wts, *, tm, n_seg, carry, seq_len, emit_vf):
    t = x.shape[0]
    tiles_per_seq = seq_len // tm if carry else 1
    n_tiles = t // tm
    hp = SSM_HEADS * SSM_HEAD_DIM
    if carry:
        hist_spec = pl.BlockSpec((1, SUBLANES, CONV_DIM), lambda i: (i // tiles_per_seq, 0, 0))
        state_spec = pl.BlockSpec((1, hp, SSM_STATE), lambda i: (i // tiles_per_seq, 0, 0))
    else:
        hist_spec = pl.BlockSpec((n_seg, SUBLANES, CONV_DIM), lambda i: (i, 0, 0))
        state_spec = pl.BlockSpec((n_seg, hp, SSM_STATE), lambda i: (i, 0, 0))
    n_seq = hist.shape[0]
    w_specs = [_resident(w.shape) for w in wts]
    w_specs[1] = _resident((D_MODEL, MAIN_COLS + LANES))
    out_shape = [
        jax.ShapeDtypeStruct((t, D_MODEL), F32),
        jax.ShapeDtypeStruct((n_seq, SUBLANES, CONV_DIM), F32),
        jax.ShapeDtypeStruct(h0.shape, F32),
    ]
    out_specs = [_rows(tm, D_MODEL), hist_spec, state_spec]
    if emit_vf:
        out_shape.append(jax.ShapeDtypeStruct((t, GMLP_WIDTH), F32))
        out_specs.append(_rows(tm, GMLP_WIDTH))
    return pl.pallas_call(
        functools.partial(_mix_kernel, n_seg, carry, tiles_per_seq),
        grid=(n_tiles,),
        in_specs=[_rows(tm, D_MODEL), hist_spec, state_spec] + w_specs,
        out_specs=out_specs,
        out_shape=out_shape,
        scratch_shapes=[pltpu.VMEM((SUBLANES, CONV_DIM), F32),
                        pltpu.VMEM((SSM_GROUPS, SSM_STATE, SSM_GROUP_WIDTH), F32)],
        compiler_params=pltpu.CompilerParams(dimension_semantics=("arbitrary",), vmem_limit_bytes=VMEM_LIMIT),
        name="mix_carry" if carry else "mix_seg",
    )(x, hist, h0, *wts)


def _ffn_kernel(x_ref, pre_w_ref, w_up_ref, w_down_ref, post_w_ref, y_ref):
    x = x_ref[...]
    h = _rms_norm(x, pre_w_ref[...]).astype(BF16)
    acc = None
    for c in range(FFN_HIDDEN // FFN_CHUNK):
        up = _dot(h, w_up_ref[:, c * FFN_CHUNK:(c + 1) * FFN_CHUNK])
        act = jnp.square(jnp.maximum(up, 0.0)).astype(BF16)
        part = _dot(act, w_down_ref[c * FFN_CHUNK:(c + 1) * FFN_CHUNK, :])
        acc = part if acc is None else acc + part
    y_ref[...] = x + _rms_norm(acc, post_w_ref[...])


def _ffn_call(x, wts, *, tm):
    t = x.shape[0]
    return pl.pallas_call(
        _ffn_kernel,
        grid=(t // tm,),
        in_specs=[_rows(tm, D_MODEL)] + [_resident(w.shape) for w in wts],
        out_specs=_rows(tm, D_MODEL),
        out_shape=jax.ShapeDtypeStruct((t, D_MODEL), F32),
        compiler_params=pltpu.CompilerParams(dimension_semantics=("arbitrary",), vmem_limit_bytes=VMEM_LIMIT),
        name="ffn",
    )(x, *wts)


def _spatial_tile(ws, bs, n, tm):
    pos = jnp.arange(n)
    mask = (pos[None, :] // CHUNK) <= (pos[:, None] // CHUNK)
    w = jnp.where(mask[None], ws[:, :n, :n], 0.0)
    reps = tm // n
    eye = jnp.eye(reps, dtype=w.dtype)
    tile = jnp.einsum("ab,gij->gaibj", eye, w).reshape(GMLP_GROUPS, tm, tm)
    bias = jnp.repeat(jnp.tile(bs[:, :n].T, (reps, 1)), GMLP_GROUP_DIM, axis=1)
    return tile.astype(BF16), bias.astype(F32)


def _layer(x, hist, h0, *, seq_len, carry, want_v, lw):
    (pre_mix_w, w_in, gmlp_ln_w, gmlp_ln_b, gmlp_ws, gmlp_bs, conv_w, conv_b, dt_bias, a_log, d_skip, ssm_norm_w,
     w_branch_a, w_branch_b, w_out, post_mix_w, pre_ffn_w, w_up, w_down, post_ffn_w) = lw
    n_seq = x.shape[0]
    t = n_seq * seq_len
    tm = TILE_M if carry else SEG_TILE_M
    n_seg = 1 if carry else tm // seq_len
    xf = x.reshape(t, D_MODEL)
    row = lambda a: a.reshape(1, -1).astype(F32)

    p_dt = MAIN_COLS
    p_ga = p_dt + SSM_HEADS
    wb = w_in.astype(BF16)
    w_tail = jnp.concatenate([wb[:, p_ga:], wb[:, p_dt:p_ga], jnp.zeros((D_MODEL, LANES - SSM_HEADS), BF16)], axis=1)
    hist8 = jnp.pad(hist.astype(F32), ((0, 0), (SUBLANES - (CONV_WIDTH - 1), 0), (0, 0)))
    h0f = h0.astype(F32).reshape(n_seq, SSM_HEADS * SSM_HEAD_DIM, SSM_STATE)
    wsp, bsp = _spatial_tile(gmlp_ws, gmlp_bs, min(seq_len, GMLP_CHUNK), tm)
    a_pad = jnp.pad(row(-jnp.exp(a_log.astype(F32))), ((0, 0), (0, LANES - SSM_HEADS)))
    expand = jnp.repeat(jnp.eye(LANES, SSM_HEADS, dtype=F32), SSM_HEAD_DIM, axis=1)
    expand3 = jnp.tile(expand, (3, 1)).astype(BF16)
    dskip_x = jnp.repeat(row(d_skip), SSM_HEAD_DIM, axis=1)
    mix_w = (row(pre_mix_w), wb, w_tail, row(gmlp_ln_w), row(gmlp_ln_b), conv_w.astype(F32), row(conv_b),
             jnp.pad(row(dt_bias), ((0, 0), (0, LANES - SSM_HEADS))),
             wsp, bsp, w_branch_a.astype(BF16), w_branch_b.astype(BF16), w_out.astype(BF16), row(ssm_norm_w),
             row(post_mix_w), a_pad, dskip_x, expand3)
    x1, cstate, h_last, *vf = _mix_call(xf, hist8, h0f, mix_w, tm=tm, n_seg=n_seg, carry=carry, seq_len=seq_len,
                                        emit_vf=want_v)

    ffn_w = (row(pre_ffn_w), w_up.astype(BF16), w_down.astype(BF16), row(post_ffn_w))
    y = _ffn_call(x1, ffn_w, tm=FFN_TILE_M)
    return (y.reshape(n_seq, seq_len, D_MODEL), cstate[:, SUBLANES - (CONV_WIDTH - 1):],
            h_last.reshape(n_seq, SSM_HEADS, SSM_HEAD_DIM, SSM_STATE).astype(h0.dtype),
            vf[0].reshape(n_seq, seq_len, GMLP_WIDTH) if want_v else None)


def kernel(x_prompt, x_sample, cache_conv, state_ssm, pre_mix_w, w_in, gmlp_ln_w, gmlp_ln_b, gmlp_ws, gmlp_bs, conv_w, conv_b, dt_bias, a_log, d_skip, ssm_norm_w, w_branch_a, w_branch_b, w_out, post_mix_w, pre_ffn_w, w_up, w_down, post_ffn_w):
    depth = w_in.shape[0]
    yp, ys = x_prompt, x_sample
    n_p = x_prompt.shape[0]
    hist0 = jnp.zeros((n_p, CONV_WIDTH - 1, CONV_DIM), x_prompt.dtype)
    h00 = jnp.zeros((n_p, SSM_HEADS, SSM_HEAD_DIM, SSM_STATE), state_ssm.dtype)
    conv_p, ssm_p, conv_s, ssm_s, v_s = [], [], [], [], []
    for l in range(depth):
        lw = (pre_mix_w[l], w_in[l], gmlp_ln_w[l], gmlp_ln_b[l], gmlp_ws[l], gmlp_bs[l], conv_w[l], conv_b[l],
              dt_bias[l], a_log[l], d_skip[l], ssm_norm_w[l], w_branch_a[l], w_branch_b[l], w_out[l],
              post_mix_w[l], pre_ffn_w[l], w_up[l], w_down[l], post_ffn_w[l])
        yp, cp, sp, _ = _layer(yp, hist0, h00, seq_len=x_prompt.shape[1], carry=True, want_v=False, lw=lw)
        ys, cs, ss, vs = _layer(ys, cache_conv[l], state_ssm[l], seq_len=x_sample.shape[1], carry=False, want_v=True,
                                lw=lw)
        conv_p.append(cp)
        ssm_p.append(sp)
        conv_s.append(cs)
        ssm_s.append(ss)
        v_s.append(vs)
    return (yp, ys, jnp.stack(conv_p), jnp.stack(ssm_p), jnp.stack(conv_s), jnp.stack(ssm_s), jnp.stack(v_s))
```

```python
import functools

import jax
import jax.numpy as jnp
from jax import lax
from jax.experimental import pallas as pl
from jax.experimental.pallas import tpu as pltpu

D_MODEL = 1024
CHUNK = 64
GMLP_CHUNK = 128
GMLP_WIDTH = 1024
GMLP_GROUPS = 8
GMLP_GROUP_DIM = GMLP_WIDTH // GMLP_GROUPS
SSM_INNER = 2 * D_MODEL
SSM_HEAD_DIM = 64
SSM_HEADS = SSM_INNER // SSM_HEAD_DIM
SSM_GROUPS = 4
SSM_HEADS_PER_GROUP = SSM_HEADS // SSM_GROUPS
SSM_STATE = 128
SSM_GROUP_WIDTH = SSM_INNER // SSM_GROUPS
BC_WIDTH = SSM_GROUPS * SSM_STATE
CONV_WIDTH = 4
CONV_DIM = SSM_INNER + 2 * BC_WIDTH
FFN_HIDDEN = 4 * D_MODEL
EPS = 1e-6
LOG2E = 1.4426950408889634

LANES = 128
SUBLANES = 8
TILE_M = 256
SEG_TILE_M = 128
FFN_TILE_M = 1024
FFN_CHUNK = 1024
SSD_BLOCK = 128
PROJ_CHUNK = 512
VMEM_LIMIT = 60 * 1024 * 1024

BF16 = jnp.bfloat16
F32 = jnp.float32

COL_U = 0
COL_V = COL_U + GMLP_WIDTH
COL_Z = COL_V + GMLP_WIDTH
COL_XBC = COL_Z + SSM_INNER
MAIN_COLS = COL_XBC + CONV_DIM
TAIL_GA = 0
TAIL_GB = TAIL_GA + D_MODEL
TAIL_DT = TAIL_GB + D_MODEL


def _dot(a, b):
    return jnp.dot(a, b, preferred_element_type=F32)


def _dot_nt(a, b):
    return lax.dot_general(a, b, (((1,), (1,)), ((), ())), preferred_element_type=F32)


def _rms_norm(x, w):
    return x * lax.rsqrt(jnp.mean(x * x, axis=-1, keepdims=True) + EPS) * w


def _sigmoid(x):
    return 1.0 / (1.0 + jnp.exp2(x * (-LOG2E)))


def _silu(x):
    return x * _sigmoid(x)


def _gelu_tanh(x):
    c = 0.7978845608028654
    return 0.5 * x * (1.0 + jnp.tanh(c * (x + 0.044715 * (x * x * x))))


def _softplus(x):
    return jnp.maximum(x, 0.0) + jnp.log(1.0 + jnp.exp(-jnp.abs(x)))


def _split3(x):
    hi = x.astype(BF16)
    r1 = x - hi.astype(F32)
    mid = r1.astype(BF16)
    lo = (r1 - mid.astype(F32)).astype(BF16)
    return jnp.concatenate([hi, mid, lo], axis=1)


def _resident(shape):
    nd = len(shape)
    return pl.BlockSpec(shape, lambda i: (0,) * nd, pipeline_mode=pl.Buffered(1))


def _rows(tm, width):
    return pl.BlockSpec((tm, width), lambda i: (i, 0))


def _mix_kernel(n_seg, carry, tiles_per_seq,
                x_ref, hist_ref, h0_ref, pre_w_ref, w_ref, w_tail_ref, ln_w_ref, ln_b_ref, conv_w_ref, conv_b_ref,
                dt_bias_ref, wsp_ref, bsp_ref, w_a_ref, w_b_ref, w_out_ref, nrm_w_ref, post_w_ref, a_ref, dskip_ref,
                exp_ref,
                y_ref, cstate_ref, hout_ref, *rest):
    tail_ref, state_ref = rest[-2:]
    vf_ref = rest[0] if len(rest) == 3 else None
    tm = x_ref.shape[0]
    seg_len = tm // n_seg
    gw = SSM_GROUP_WIDTH
    cc = PROJ_CHUNK

    if carry:
        first = (pl.program_id(0) % tiles_per_seq) == 0

        @pl.when(first)
        def _():
            for g in range(SSM_GROUPS):
                state_ref[g] = h0_ref[0, g * gw:(g + 1) * gw, :].T

    x = x_ref[...]
    h = _rms_norm(x, pre_w_ref[...]).astype(BF16)

    def proj(col0, c, w=w_ref):
        return _dot(h, w[:, col0 + c * cc:col0 + (c + 1) * cc])

    def conv_chunk(c):
        cols = slice(c * cc, (c + 1) * cc)
        raw = proj(COL_XBC, c)
        cw = conv_w_ref[:, cols]
        outs = []
        for s in range(n_seg):
            seg = raw[s * seg_len:(s + 1) * seg_len]
            if carry:
                prev = jnp.where(first, hist_ref[0, :, cols], tail_ref[:, cols])
            else:
                prev = hist_ref[s, :, cols]
            xp = jnp.concatenate([prev, seg], axis=0)
            acc = conv_b_ref[:, cols] + seg * cw[CONV_WIDTH - 1:CONV_WIDTH]
            for k in range(1, CONV_WIDTH):
                shifted = pltpu.roll(xp, k, 0)[SUBLANES:]
                acc = acc + shifted * cw[CONV_WIDTH - 1 - k:CONV_WIDTH - k]
            outs.append(_silu(acc))
            cstate_ref[s, :, cols] = seg[seg_len - SUBLANES:]
        if carry:
            tail_ref[:, cols] = raw[tm - SUBLANES:]
        return outs[0] if n_seg == 1 else jnp.concatenate(outs, axis=0)

    xbc = jnp.concatenate([conv_chunk(c) for c in range(CONV_DIM // cc)], axis=1)
    xs = xbc[:, :SSM_INNER]
    bm = xbc[:, SSM_INNER:SSM_INNER + BC_WIDTH].astype(BF16)
    cm = xbc[:, SSM_INNER + BC_WIDTH:].astype(BF16)
    ga = jnp.concatenate([_sigmoid(proj(TAIL_GA, c, w_tail_ref)) for c in range(D_MODEL // cc)], axis=1)
    gb = jnp.concatenate([_sigmoid(proj(TAIL_GB, c, w_tail_ref)) for c in range(D_MODEL // cc)], axis=1)
    zs = jnp.concatenate([_silu(proj(COL_Z, c)) for c in range(SSM_INNER // cc)], axis=1)
    u = jnp.concatenate([_gelu_tanh(proj(COL_U, c)) for c in range(GMLP_WIDTH // cc)], axis=1)
    v = jnp.concatenate([_gelu_tanh(proj(COL_V, c)) for c in range(GMLP_WIDTH // cc)], axis=1)
    mu = jnp.mean(v, axis=-1, keepdims=True)
    vc = v - mu
    var = jnp.mean(vc * vc, axis=-1, keepdims=True)
    vn = vc * lax.rsqrt(var + EPS) * ln_w_ref[...] + ln_b_ref[...]
    if vf_ref is not None:
        vf_ref[...] = vn
    dt = _softplus(_dot(h, w_tail_ref[:, TAIL_DT:TAIL_DT + LANES]) + dt_bias_ref[...])

    vb = vn.astype(BF16)
    sp = [_dot(wsp_ref[g], vb[:, g * GMLP_GROUP_DIM:(g + 1) * GMLP_GROUP_DIM]) for g in range(GMLP_GROUPS)]
    o_a = u * (jnp.concatenate(sp, axis=1) + bsp_ref[...])
    y_a = _dot(o_a.astype(BF16), w_a_ref[...])

    qd = SSD_BLOCK
    qs = min(qd, seg_len)
    blocks_per_seg = seg_len // qs
    seg_shift = seg_len.bit_length() - 1
    row = lax.broadcasted_iota(jnp.int32, (tm, tm), 0)
    col = lax.broadcasted_iota(jnp.int32, (tm, tm), 1)
    causal = (row >= col) & ((row >> seg_shift) == (col >> seg_shift))
    tri = causal[:qd, :qd]
    dta = dt * a_ref[...]
    ones_tri = jnp.where(causal, 1.0, 0.0).astype(BF16)
    acum3 = _dot(ones_tri, _split3(dta))
    acum = acum3[:, :LANES] + acum3[:, LANES:2 * LANES] + acum3[:, 2 * LANES:]
    acum2 = acum * LOG2E
    acum2_t = acum2.T
    acum_x = _dot(_split3(acum), exp_ref[...])
    dt_x = _dot(_split3(dt), exp_ref[...])
    xdt = xs * dt_x
    xdt_b = xdt.astype(BF16)
    e_off, xw = [], []
    for b in range(tm // qs):
        blk = acum_x[b * qs:(b + 1) * qs]
        if b % blocks_per_seg:
            blk = blk - acum_x[b * qs - 1:b * qs]
        e_off.append(jnp.exp(blk))
        xw.append((xdt[b * qs:(b + 1) * qs] * jnp.exp(blk[qs - 1:qs] - blk)).astype(BF16))
    low_half = lax.broadcasted_iota(jnp.int32, (qd, LANES), 1) < SSM_HEAD_DIM

    y_groups = []
    for g in range(SSM_GROUPS):
        gcols = slice(g * gw, (g + 1) * gw)
        y_blocks = []
        h_run = None
        for d in range(tm // qd):
            d0, d1 = d * qd, (d + 1) * qd
            cg = cm[d0:d1, g * SSM_STATE:(g + 1) * SSM_STATE]
            bg = bm[d0:d1, g * SSM_STATE:(g + 1) * SSM_STATE]
            cb = jnp.where(tri, _dot_nt(cg, bg), 0.0)
            y_pairs = []
            for j in range(SSM_HEADS_PER_GROUP // 2):
                lhs = []
                for hh in (2 * j, 2 * j + 1):
                    hd = g * SSM_HEADS_PER_GROUP + hh
                    seg = acum2[d0:d1, hd:hd + 1] - acum2_t[hd:hd + 1, d0:d1]
                    lhs.append((jnp.exp2(jnp.minimum(seg, 0.0)) * cb).astype(BF16))
                c0 = g * gw + j * LANES
                pair = xdt_b[d0:d1, c0:c0 + LANES]
                zero = jnp.zeros_like(pair)
                rhs = jnp.concatenate([jnp.where(low_half, pair, zero), jnp.where(low_half, zero, pair)], axis=0)
                y_pairs.append(_dot(jnp.concatenate(lhs, axis=1), rhs))
            y_diag = jnp.concatenate(y_pairs, axis=1)
            y_off = []
            for b in range(d * (qd // qs), (d + 1) * (qd // qs)):
                r0, r1 = b * qs - d0, (b + 1) * qs - d0
                s = b // blocks_per_seg
                if b % blocks_per_seg == 0:
                    h_run = state_ref[g] if carry else h0_ref[s, gcols, :].T
                eo = e_off[b][:, gcols]
                y_off.append(_dot(cg[r0:r1], h_run.astype(BF16)) * eo)
                bt = bg[r0:r1].astype(F32).T.astype(BF16)
                h_run = h_run * eo[qs - 1:qs] + _dot(bt, xw[b][:, gcols])
                if (b + 1) % blocks_per_seg == 0:
                    if carry:
                        state_ref[g] = h_run
                    else:
                        hout_ref[s, gcols, :] = h_run.T
            y_blocks.append(y_diag + (y_off[0] if len(y_off) == 1 else jnp.concatenate(y_off, axis=0)))
        y_groups.append(y_blocks[0] if len(y_blocks) == 1 else jnp.concatenate(y_blocks, axis=0))
    y = jnp.concatenate(y_groups, axis=1) + dskip_ref[...] * xs

    gz = y * zs
    nw = nrm_w_ref[...]
    o_b = []
    for g in range(SSM_GROUPS):
        blk = gz[:, g * gw:(g + 1) * gw]
        blk = blk * lax.rsqrt(jnp.mean(blk * blk, axis=-1, keepdims=True) + EPS)
        o_b.append((blk * nw[:, g * gw:(g + 1) * gw]).astype(BF16))
    y_b = _dot(jnp.concatenate(o_b, axis=1), w_b_ref[...])

    merged = ga * y_a + gb * y_b
    mixed = _dot(merged.astype(BF16), w_out_ref[...])
    y_ref[...] = x + _rms_norm(mixed, post_w_ref[...])

    if carry:
        @pl.when((pl.program_id(0) % tiles_per_seq) == tiles_per_seq - 1)
        def _():
            for g in range(SSM_GROUPS):
                hout_ref[0, g * gw:(g + 1) * gw, :] = state_ref[g].T


def _mix_call(x, hist, h0, wts, *, tm, n_seg, carry, seq_len, emit_vf):
    t = x.shape[0]
    tiles_per_seq = seq_len // tm if carry else 1
    n_tiles = t // tm
    hp = SSM_HEADS * SSM_HEAD_DIM
    if carry:
        hist_spec = pl.BlockSpec((1, SUBLANES, CONV_DIM), lambda i: (i // tiles_per_seq, 0, 0))
        state_spec = pl.BlockSpec((1, hp, SSM_STATE), lambda i: (i // tiles_per_seq, 0, 0))
    else:
        hist_spec = pl.BlockSpec((n_seg, SUBLANES, CONV_DIM), lambda i: (i, 0, 0))
        state_spec = pl.BlockSpec((n_seg, hp, SSM_STATE), lambda i: (i, 0, 0))
    n_seq = hist.shape[0]
    w_specs = [_resident(w.shape) for w in wts]
    w_specs[1] = _resident((D_MODEL, MAIN_COLS + LANES))
    out_shape = [
        jax.ShapeDtypeStruct((t, D_MODEL), F32),
        jax.ShapeDtypeStruct((n_seq, SUBLANES, CONV_DIM), F32),
        jax.ShapeDtypeStruct(h0.shape, F32),
    ]
    out_specs = [_rows(tm, D_MODEL), hist_spec, state_spec]
    if emit_vf:
        out_shape.append(jax.ShapeDtypeStruct((t, GMLP_WIDTH), F32))
        out_specs.append(_rows(tm, GMLP_WIDTH))
    return pl.pallas_call(
        functools.partial(_mix_kernel, n_seg, carry, tiles_per_seq),
        grid=(n_tiles,),
        in_specs=[_rows(tm, D_MODEL), hist_spec, state_spec] + w_specs,
        out_specs=out_specs,
        out_shape=out_shape,
        scratch_shapes=[pltpu.VMEM((SUBLANES, CONV_DIM), F32),
                        pltpu.VMEM((SSM_GROUPS, SSM_STATE, SSM_GROUP_WIDTH), F32)],
        compiler_params=pltpu.CompilerParams(dimension_semantics=("arbitrary",), vmem_limit_bytes=VMEM_LIMIT),
        name="mix_carry" if carry else "mix_seg",
    )(x, hist, h0, *wts)


def _ffn_kernel(x_ref, pre_w_ref, w_up_ref, w_down_ref, post_w_ref, y_ref):
    x = x_ref[...]
    h = _rms_norm(x, pre_w_ref[...]).astype(BF16)
    acc = None
    for c in range(FFN_HIDDEN // FFN_CHUNK):
        up = _dot(h, w_up_ref[:, c * FFN_CHUNK:(c + 1) * FFN_CHUNK])
        act = jnp.square(jnp.maximum(up, 0.0)).astype(BF16)
        part = _dot(act, w_down_ref[c * FFN_CHUNK:(c + 1) * FFN_CHUNK, :])
        acc = part if acc is None else acc + part
    y_ref[...] = x + _rms_norm(acc, post_w_ref[...])


def _ffn_call(x, wts, *, tm):
    t = x.shape[0]
    return pl.pallas_call(
        _ffn_kernel,
        grid=(t // tm,),
        in_specs=[_rows(tm, D_MODEL)] + [_resident(w.shape) for w in wts],
        out_specs=_rows(tm, D_MODEL),
        out_shape=jax.ShapeDtypeStruct((t, D_MODEL), F32),
        compiler_params=pltpu.CompilerParams(dimension_semantics=("arbitrary",), vmem_limit_bytes=VMEM_LIMIT),
        name="ffn",
    )(x, *wts)


def _spatial_tile(ws, bs, n, tm):
    pos = jnp.arange(n)
    mask = (pos[None, :] // CHUNK) <= (pos[:, None] // CHUNK)
    w = jnp.where(mask[None], ws[:, :n, :n], 0.0)
    reps = tm // n
    eye = jnp.eye(reps, dtype=w.dtype)
    tile = jnp.einsum("ab,gij->gaibj", eye, w).reshape(GMLP_GROUPS, tm, tm)
    bias = jnp.repeat(jnp.tile(bs[:, :n].T, (reps, 1)), GMLP_GROUP_DIM, axis=1)
    return tile.astype(BF16), bias.astype(F32)


def _layer(x, hist, h0, *, seq_len, carry, want_v, lw):
    (pre_mix_w, w_in, gmlp_ln_w, gmlp_ln_b, gmlp_ws, gmlp_bs, conv_w, conv_b, dt_bias, a_log, d_skip, ssm_norm_w,
     w_branch_a, w_branch_b, w_out, post_mix_w, pre_ffn_w, w_up, w_down, post_ffn_w) = lw
    n_seq = x.shape[0]
    t = n_seq * seq_len
    tm = TILE_M if carry else SEG_TILE_M
    n_seg = 1 if carry else tm // seq_len
    xf = x.reshape(t, D_MODEL)
    row = lambda a: a.reshape(1, -1).astype(F32)

    p_dt = MAIN_COLS
    p_ga = p_dt + SSM_HEADS
    wb = w_in.astype(BF16)
    w_tail = jnp.concatenate([wb[:, p_ga:], wb[:, p_dt:p_ga], jnp.zeros((D_MODEL, LANES - SSM_HEADS), BF16)], axis=1)
    hist8 = jnp.pad(hist.astype(F32), ((0, 0), (SUBLANES - (CONV_WIDTH - 1), 0), (0, 0)))
    h0f = h0.astype(F32).reshape(n_seq, SSM_HEADS * SSM_HEAD_DIM, SSM_STATE)
    wsp, bsp = _spatial_tile(gmlp_ws, gmlp_bs, min(seq_len, GMLP_CHUNK), tm)
    a_pad = jnp.pad(row(-jnp.exp(a_log.astype(F32))), ((0, 0), (0, LANES - SSM_HEADS)))
    expand = jnp.repeat(jnp.eye(LANES, SSM_HEADS, dtype=F32), SSM_HEAD_DIM, axis=1)
    expand3 = jnp.tile(expand, (3, 1)).astype(BF16)
    dskip_x = jnp.repeat(row(d_skip), SSM_HEAD_DIM, axis=1)
    mix_w = (row(pre_mix_w), wb, w_tail, row(gmlp_ln_w), row(gmlp_ln_b), conv_w.astype(F32), row(conv_b),
             jnp.pad(row(dt_bias), ((0, 0), (0, LANES - SSM_HEADS))),
             wsp, bsp, w_branch_a.astype(BF16), w_branch_b.astype(BF16), w_out.astype(BF16), row(ssm_norm_w),
             row(post_mix_w), a_pad, dskip_x, expand3)
    x1, cstate, h_last, *vf = _mix_call(xf, hist8, h0f, mix_w, tm=tm, n_seg=n_seg, carry=carry, seq_len=seq_len,
                                        emit_vf=want_v)

    ffn_w = (row(pre_ffn_w), w_up.astype(BF16), w_down.astype(BF16), row(post_ffn_w))
    y = _ffn_call(x1, ffn_w, tm=FFN_TILE_M)
    return (y.reshape(n_seq, seq_len, D_MODEL), cstate[:, SUBLANES - (CONV_WIDTH - 1):],
            h_last.reshape(n_seq, SSM_HEADS, SSM_HEAD_DIM, SSM_STATE).astype(h0.dtype),
            vf[0].reshape(n_seq, seq_len, GMLP_WIDTH) if want_v else None)


def kernel(x_prompt, x_sample, cache_conv, state_ssm, pre_mix_w, w_in, gmlp_ln_w, gmlp_ln_b, gmlp_ws, gmlp_bs, conv_w, conv_b, dt_bias, a_log, d_skip, ssm_norm_w, w_branch_a, w_branch_b, w_out, post_mix_w, pre_ffn_w, w_up, w_down, post_ffn_w):
    depth = w_in.shape[0]
    yp, ys = x_prompt, x_sample
    n_p = x_prompt.shape[0]
    hist0 = jnp.zeros((n_p, CONV_WIDTH - 1, CONV_DIM), x_prompt.dtype)
    h00 = jnp.zeros((n_p, SSM_HEADS, SSM_HEAD_DIM, SSM_STATE), state_ssm.dtype)
    conv_p, ssm_p, conv_s, ssm_s, v_s = [], [], [], [], []
    for l in range(depth):
        lw = (pre_mix_w[l], w_in[l], gmlp_ln_w[l], gmlp_ln_b[l], gmlp_ws[l], gmlp_bs[l], conv_w[l], conv_b[l],
              dt_bias[l], a_log[l], d_skip[l], ssm_norm_w[l], w_branch_a[l], w_branch_b[l], w_out[l],
              post_mix_w[l], pre_ffn_w[l], w_up[l], w_down[l], post_ffn_w[l])
        yp, cp, sp, _ = _layer(yp, hist0, h00, seq_len=x_prompt.shape[1], carry=True, want_v=False, lw=lw)
        ys, cs, ss, vs = _layer(ys, cache_conv[l], state_ssm[l], seq_len=x_sample.shape[1], carry=False, want_v=True,
                                lw=lw)
        conv_p.append(cp)
        ssm_p.append(sp)
        conv_s.append(cs)
        ssm_s.append(ss)
        v_s.append(vs)
    return (yp, ys, jnp.stack(conv_p), jnp.stack(ssm_p), jnp.stack(conv_s), jnp.stack(ssm_s), jnp.stack(v_s))
```

```python
import functools

import jax
import jax.numpy as jnp
import numpy as np
from jax import lax
from jax.experimental import pallas as pl
from jax.experimental.pallas import tpu as pltpu

D_MODEL = 1024
CHUNK = 64
GMLP_CHUNK = 128
GMLP_WIDTH = 1024
GMLP_GROUPS = 8
GMLP_GROUP_DIM = GMLP_WIDTH // GMLP_GROUPS
SSM_INNER = 2 * D_MODEL
SSM_HEAD_DIM = 64
SSM_HEADS = SSM_INNER // SSM_HEAD_DIM
SSM_GROUPS = 4
SSM_HEADS_PER_GROUP = SSM_HEADS // SSM_GROUPS
SSM_STATE = 128
SSM_GROUP_WIDTH = SSM_INNER // SSM_GROUPS
BC_WIDTH = SSM_GROUPS * SSM_STATE
CONV_WIDTH = 4
CONV_DIM = SSM_INNER + 2 * BC_WIDTH
FFN_HIDDEN = 4 * D_MODEL
EPS = 1e-6
LOG2E = 1.4426950408889634

LANES = 128
SUBLANES = 8
TILE_M = 256
SEG_TILE_M = 128
FFN_TILE_M = 1024
FFN_CHUNK = 1024
SSD_BLOCK = 128
PROJ_CHUNK = 512
VMEM_LIMIT = 60 * 1024 * 1024

BF16 = jnp.bfloat16
F32 = jnp.float32

COL_U = 0
COL_V = COL_U + GMLP_WIDTH
COL_Z = COL_V + GMLP_WIDTH
COL_XBC = COL_Z + SSM_INNER
MAIN_COLS = COL_XBC + CONV_DIM
TAIL_GA = 0
TAIL_GB = TAIL_GA + D_MODEL
TAIL_DT = TAIL_GB + D_MODEL


def _dot(a, b):
    return jnp.dot(a, b, preferred_element_type=F32)


def _dot_nt(a, b):
    return lax.dot_general(a, b, (((1,), (1,)), ((), ())), preferred_element_type=F32)


def _rms_norm(x, w):
    return x * lax.rsqrt(jnp.mean(x * x, axis=-1, keepdims=True) + EPS) * w


def _sigmoid(x):
    return 1.0 / (1.0 + jnp.exp2(x * (-LOG2E)))


def _silu(x):
    return x * _sigmoid(x)


def _gelu_tanh(x):
    c = 0.7978845608028654
    return 0.5 * x * (1.0 + jnp.tanh(c * (x + 0.044715 * (x * x * x))))


def _softplus(x):
    return jnp.maximum(x, 0.0) + jnp.log(1.0 + jnp.exp(-jnp.abs(x)))


def _split3(x):
    hi = x.astype(BF16)
    r1 = x - hi.astype(F32)
    mid = r1.astype(BF16)
    lo = (r1 - mid.astype(F32)).astype(BF16)
    return jnp.concatenate([hi, mid, lo], axis=1)


def _resident(shape):
    nd = len(shape)
    return pl.BlockSpec(shape, lambda i: (0,) * nd, pipeline_mode=pl.Buffered(1))


def _rows(tm, width):
    return pl.BlockSpec((tm, width), lambda i: (i, 0))


def _mix_kernel(n_seg, carry, tiles_per_seq,
                x_ref, hist_ref, h0_ref, pre_w_ref, w_ref, w_tail_ref, ln_w_ref, ln_b_ref, conv_w_ref, conv_b_ref,
                dt_bias_ref, wsp_ref, bsp_ref, w_a_ref, w_b_ref, w_out_ref, nrm_w_ref, post_w_ref, a_ref, dskip_ref,
                exp_ref,
                y_ref, cstate_ref, hout_ref, *rest):
    tail_ref, state_ref = rest[-2:]
    vf_ref = rest[0] if len(rest) == 3 else None
    tm = x_ref.shape[0]
    seg_len = tm // n_seg
    gw = SSM_GROUP_WIDTH
    cc = PROJ_CHUNK

    if carry:
        first = (pl.program_id(0) % tiles_per_seq) == 0

        @pl.when(first)
        def _():
            for g in range(SSM_GROUPS):
                state_ref[g] = h0_ref[0, g * gw:(g + 1) * gw, :].T

    x = x_ref[...]
    h = _rms_norm(x, pre_w_ref[...]).astype(BF16)

    def proj(col0, c, w=w_ref):
        return _dot(h, w[:, col0 + c * cc:col0 + (c + 1) * cc])

    def conv_chunk(c):
        cols = slice(c * cc, (c + 1) * cc)
        raw = proj(COL_XBC, c)
        cw = conv_w_ref[:, cols]
        outs = []
        for s in range(n_seg):
            seg = raw[s * seg_len:(s + 1) * seg_len]
            if carry:
                prev = jnp.where(first, hist_ref[0, :, cols], tail_ref[:, cols])
            else:
                prev = hist_ref[s, :, cols]
            xp = jnp.concatenate([prev, seg], axis=0)
            acc = conv_b_ref[:, cols] + seg * cw[CONV_WIDTH - 1:CONV_WIDTH]
            for k in range(1, CONV_WIDTH):
                shifted = pltpu.roll(xp, k, 0)[SUBLANES:]
                acc = acc + shifted * cw[CONV_WIDTH - 1 - k:CONV_WIDTH - k]
            outs.append(_silu(acc))
            cstate_ref[s, :, cols] = seg[seg_len - SUBLANES:]
        if carry:
            tail_ref[:, cols] = raw[tm - SUBLANES:]
        return outs[0] if n_seg == 1 else jnp.concatenate(outs, axis=0)

    xbc = jnp.concatenate([conv_chunk(c) for c in range(CONV_DIM // cc)], axis=1)
    xs = xbc[:, :SSM_INNER]
    bm = xbc[:, SSM_INNER:SSM_INNER + BC_WIDTH].astype(BF16)
    cm = xbc[:, SSM_INNER + BC_WIDTH:].astype(BF16)
    ga = jnp.concatenate([_sigmoid(proj(TAIL_GA, c, w_tail_ref)) for c in range(D_MODEL // cc)], axis=1)
    gb = jnp.concatenate([_sigmoid(proj(TAIL_GB, c, w_tail_ref)) for c in range(D_MODEL // cc)], axis=1)
    zs = jnp.concatenate([_silu(proj(COL_Z, c)) for c in range(SSM_INNER // cc)], axis=1)
    u = jnp.concatenate([_gelu_tanh(proj(COL_U, c)) for c in range(GMLP_WIDTH // cc)], axis=1)
    v = jnp.concatenate([_gelu_tanh(proj(COL_V, c)) for c in range(GMLP_WIDTH // cc)], axis=1)
    mu = jnp.mean(v, axis=-1, keepdims=True)
    vc = v - mu
    var = jnp.mean(vc * vc, axis=-1, keepdims=True)
    vn = vc * lax.rsqrt(var + EPS) * ln_w_ref[...] + ln_b_ref[...]
    if vf_ref is not None:
        vf_ref[...] = vn
    dt = _softplus(_dot(h, w_tail_ref[:, TAIL_DT:TAIL_DT + LANES]) + dt_bias_ref[...])

    vb = vn.astype(BF16)
    sp, y_a_cols = [], []
    a_cols = D_MODEL // SSM_GROUPS
    per_step = GMLP_GROUPS // (SSM_GROUPS // 2)

    def branch_a_piece(g):
        if g < SSM_GROUPS // 2:
            sp.extend(_dot(wsp_ref[k], vb[:, k * GMLP_GROUP_DIM:(k + 1) * GMLP_GROUP_DIM])
                      for k in range(per_step * g, per_step * (g + 1)))
            return
        if not y_a_cols:
            y_a_cols.append((u * (jnp.concatenate(sp, axis=1) + bsp_ref[...])).astype(BF16))
        k0 = 2 * (g - SSM_GROUPS // 2)
        y_a_cols.extend(_dot(y_a_cols[0], w_a_ref[:, k * a_cols:(k + 1) * a_cols]) for k in (k0, k0 + 1))

    qd = SSD_BLOCK
    qs = min(qd, seg_len)
    blocks_per_seg = seg_len // qs
    seg_shift = seg_len.bit_length() - 1
    row = lax.broadcasted_iota(jnp.int32, (tm, tm), 0)
    col = lax.broadcasted_iota(jnp.int32, (tm, tm), 1)
    causal = (row >= col) & ((row >> seg_shift) == (col >> seg_shift))
    tri = causal[:qd, :qd]
    dta = dt * a_ref[...]
    ones_tri = jnp.where(causal, 1.0, 0.0).astype(BF16)
    acum3 = _dot(ones_tri, _split3(dta))
    acum = acum3[:, :LANES] + acum3[:, LANES:2 * LANES] + acum3[:, 2 * LANES:]
    acum2 = acum * LOG2E
    acum2_t = acum2.T
    acum_x = _dot(_split3(acum), exp_ref[...])
    dt_x = _dot(_split3(dt), exp_ref[...])
    xdt = xs * dt_x
    xdt_b = xdt.astype(BF16)
    e_off, xw = [], []
    for b in range(tm // qs):
        blk = acum_x[b * qs:(b + 1) * qs]
        if b % blocks_per_seg:
            blk = blk - acum_x[b * qs - 1:b * qs]
        e_off.append(jnp.exp(blk))
        xw.append((xdt[b * qs:(b + 1) * qs] * jnp.exp(blk[qs - 1:qs] - blk)).astype(BF16))
    low_half = lax.broadcasted_iota(jnp.int32, (qd, LANES), 1) < SSM_HEAD_DIM

    y_groups = []
    for g in range(SSM_GROUPS):
        gcols = slice(g * gw, (g + 1) * gw)
        y_blocks = []
        h_run = None
        for d in range(tm // qd):
            d0, d1 = d * qd, (d + 1) * qd
            cg = cm[d0:d1, g * SSM_STATE:(g + 1) * SSM_STATE]
            bg = bm[d0:d1, g * SSM_STATE:(g + 1) * SSM_STATE]
            cb = jnp.where(tri, _dot_nt(cg, bg), 0.0)
            y_pairs = []
            for j in range(SSM_HEADS_PER_GROUP // 2):
                lhs = []
                for hh in (2 * j, 2 * j + 1):
                    hd = g * SSM_HEADS_PER_GROUP + hh
                    seg = acum2[d0:d1, hd:hd + 1] - acum2_t[hd:hd + 1, d0:d1]
                    lhs.append((jnp.exp2(jnp.minimum(seg, 0.0)) * cb).astype(BF16))
                c0 = g * gw + j * LANES
                pair = xdt_b[d0:d1, c0:c0 + LANES]
                zero = jnp.zeros_like(pair)
                rhs = jnp.concatenate([jnp.where(low_half, pair, zero), jnp.where(low_half, zero, pair)], axis=0)
                y_pairs.append(_dot(jnp.concatenate(lhs, axis=1), rhs))
            y_diag = jnp.concatenate(y_pairs, axis=1)
            y_off = []
            for b in range(d * (qd // qs), (d + 1) * (qd // qs)):
                r0, r1 = b * qs - d0, (b + 1) * qs - d0
                s = b // blocks_per_seg
                if b % blocks_per_seg == 0:
                    h_run = state_ref[g] if carry else h0_ref[s, gcols, :].T
                eo = e_off[b][:, gcols]
                y_off.append(_dot(cg[r0:r1], h_run.astype(BF16)) * eo)
                bt = bg[r0:r1].astype(F32).T.astype(BF16)
                h_run = h_run * eo[qs - 1:qs] + _dot(bt, xw[b][:, gcols])
                if (b + 1) % blocks_per_seg == 0:
                    if carry:
                        state_ref[g] = h_run
                    else:
                        hout_ref[s, gcols, :] = h_run.T
            y_blocks.append(y_diag + (y_off[0] if len(y_off) == 1 else jnp.concatenate(y_off, axis=0)))
        y_groups.append(y_blocks[0] if len(y_blocks) == 1 else jnp.concatenate(y_blocks, axis=0))
        branch_a_piece(g)
    y_a = jnp.concatenate(y_a_cols[1:], axis=1)
    y = jnp.concatenate(y_groups, axis=1) + dskip_ref[...] * xs

    gz = y * zs
    nw = nrm_w_ref[...]
    o_b = []
    for g in range(SSM_GROUPS):
        blk = gz[:, g * gw:(g + 1) * gw]
        blk = blk * lax.rsqrt(jnp.mean(blk * blk, axis=-1, keepdims=True) + EPS)
        o_b.append((blk * nw[:, g * gw:(g + 1) * gw]).astype(BF16))
    y_b = _dot(jnp.concatenate(o_b, axis=1), w_b_ref[...])

    merged = ga * y_a + gb * y_b
    mixed = _dot(merged.astype(BF16), w_out_ref[...])
    y_ref[...] = x + _rms_norm(mixed, post_w_ref[...])

    if carry:
        @pl.when((pl.program_id(0) % tiles_per_seq) == tiles_per_seq - 1)
        def _():
            for g in range(SSM_GROUPS):
                hout_ref[0, g * gw:(g + 1) * gw, :] = state_ref[g].T


def _mix_call(x, hist, h0, wts, *, tm, n_seg, carry, seq_len, emit_vf):
    t = x.shape[0]
    tiles_per_seq = seq_len // tm if carry else 1
    n_tiles = t // tm
    hp = SSM_HEADS * SSM_HEAD_DIM
    if carry:
        hist_spec = pl.BlockSpec((1, SUBLANES, CONV_DIM), lambda i: (i // tiles_per_seq, 0, 0))
        state_spec = pl.BlockSpec((1, hp, SSM_STATE), lambda i: (i // tiles_per_seq, 0, 0))
    else:
        hist_spec = pl.BlockSpec((n_seg, SUBLANES, CONV_DIM), lambda i: (i, 0, 0))
        state_spec = pl.BlockSpec((n_seg, hp, SSM_STATE), lambda i: (i, 0, 0))
    n_seq = hist.shape[0]
    w_specs = [_resident(w.shape) for w in wts]
    w_specs[1] = _resident((D_MODEL, MAIN_COLS + LANES))
    out_shape = [
        jax.ShapeDtypeStruct((t, D_MODEL), F32),
        jax.ShapeDtypeStruct((n_seq, SUBLANES, CONV_DIM), F32),
        jax.ShapeDtypeStruct(h0.shape, F32),
    ]
    out_specs = [_rows(tm, D_MODEL), hist_spec, state_spec]
    if emit_vf:
        out_shape.append(jax.ShapeDtypeStruct((t, GMLP_WIDTH), F32))
        out_specs.append(_rows(tm, GMLP_WIDTH))
    return pl.pallas_call(
        functools.partial(_mix_kernel, n_seg, carry, tiles_per_seq),
        grid=(n_tiles,),
        in_specs=[_rows(tm, D_MODEL), hist_spec, state_spec] + w_specs,
        out_specs=out_specs,
        out_shape=out_shape,
        scratch_shapes=[pltpu.VMEM((SUBLANES, CONV_DIM), F32),
                        pltpu.VMEM((SSM_GROUPS, SSM_STATE, SSM_GROUP_WIDTH), F32)],
        compiler_params=pltpu.CompilerParams(dimension_semantics=("arbitrary",), vmem_limit_bytes=VMEM_LIMIT),
        name="mix_carry" if carry else "mix_seg",
    )(x, hist, h0, *wts)


def _ffn_kernel(x_ref, pre_w_ref, w_up_ref, w_down_ref, post_w_ref, y_ref):
    x = x_ref[...]
    h = _rms_norm(x, pre_w_ref[...]).astype(BF16)
    acc = None
    for c in range(FFN_HIDDEN // FFN_CHUNK):
        up = _dot(h, w_up_ref[:, c * FFN_CHUNK:(c + 1) * FFN_CHUNK].astype(BF16))
        act = jnp.square(jnp.maximum(up, 0.0)).astype(BF16)
        part = _dot(act, w_down_ref[c * FFN_CHUNK:(c + 1) * FFN_CHUNK, :].astype(BF16))
        acc = part if acc is None else acc + part
    y_ref[...] = x + _rms_norm(acc, post_w_ref[...])


def _ffn_call(x, wts, *, tm):
    t = x.shape[0]
    return pl.pallas_call(
        _ffn_kernel,
        grid=(t // tm,),
        in_specs=[_rows(tm, D_MODEL)] + [_resident(w.shape) for w in wts],
        out_specs=_rows(tm, D_MODEL),
        out_shape=jax.ShapeDtypeStruct((t, D_MODEL), F32),
        compiler_params=pltpu.CompilerParams(dimension_semantics=("arbitrary",), vmem_limit_bytes=VMEM_LIMIT),
        name="ffn",
    )(x, *wts)


def _spatial_tile(ws, bs, n, tm):
    pos = jnp.arange(n)
    mask = (pos[None, :] // CHUNK) <= (pos[:, None] // CHUNK)
    w = jnp.where(mask[None], ws[:, :n, :n], 0.0)
    reps = tm // n
    eye = jnp.eye(reps, dtype=w.dtype)
    tile = jnp.einsum("ab,gij->gaibj", eye, w).reshape(GMLP_GROUPS, tm, tm)
    bias = jnp.repeat(jnp.tile(bs[:, :n].T, (reps, 1)), GMLP_GROUP_DIM, axis=1)
    return tile.astype(BF16), bias.astype(F32)


def _layer(x, hist, h0, *, seq_len, carry, want_v, lw):
    (pre_mix_w, w_in, gmlp_ln_w, gmlp_ln_b, gmlp_ws, gmlp_bs, conv_w, conv_b, dt_bias, a_log, d_skip, ssm_norm_w,
     w_branch_a, w_branch_b, w_out, post_mix_w, pre_ffn_w, w_up, w_down, post_ffn_w) = lw
    n_seq = x.shape[0]
    t = n_seq * seq_len
    tm = TILE_M if carry else SEG_TILE_M
    n_seg = 1 if carry else tm // seq_len
    assert t % tm == 0 and t % FFN_TILE_M == 0 and tm % SSD_BLOCK == 0, (t, tm)
    if carry:
        assert seq_len % tm == 0 and seq_len % GMLP_CHUNK == 0, seq_len
    else:
        assert tm % seq_len == 0 and SUBLANES <= seq_len <= GMLP_CHUNK and seq_len & (seq_len - 1) == 0, seq_len
    xf = x.reshape(t, D_MODEL)
    row = lambda a: a.reshape(1, -1).astype(F32)

    p_dt = MAIN_COLS
    p_ga = p_dt + SSM_HEADS
    wb = w_in.astype(BF16)
    w_tail = jnp.concatenate([w_in[:, p_ga:], w_in[:, p_dt:p_ga],
                              jnp.zeros((D_MODEL, LANES - SSM_HEADS), w_in.dtype)], axis=1).astype(BF16)
    hist8 = jnp.pad(hist.astype(F32), ((0, 0), (SUBLANES - (CONV_WIDTH - 1), 0), (0, 0)))
    h0f = h0.astype(F32).reshape(n_seq, SSM_HEADS * SSM_HEAD_DIM, SSM_STATE)
    wsp, bsp = _spatial_tile(gmlp_ws, gmlp_bs, min(seq_len, GMLP_CHUNK), tm)
    a_pad = jnp.pad(row(-jnp.exp(a_log.astype(F32))), ((0, 0), (0, LANES - SSM_HEADS)))
    expand = np.repeat(np.eye(LANES, SSM_HEADS, dtype=np.float32), SSM_HEAD_DIM, axis=1)
    expand3 = jnp.asarray(np.tile(expand, (3, 1)), BF16)
    dskip_x = jnp.repeat(row(d_skip), SSM_HEAD_DIM, axis=1)
    mix_w = (row(pre_mix_w), wb, w_tail, row(gmlp_ln_w), row(gmlp_ln_b), conv_w.astype(F32), row(conv_b),
             jnp.pad(row(dt_bias), ((0, 0), (0, LANES - SSM_HEADS))),
             wsp, bsp, w_branch_a.astype(BF16), w_branch_b.astype(BF16), w_out.astype(BF16), row(ssm_norm_w),
             row(post_mix_w), a_pad, dskip_x, expand3)
    x1, cstate, h_last, *vf = _mix_call(xf, hist8, h0f, mix_w, tm=tm, n_seg=n_seg, carry=carry, seq_len=seq_len,
                                        emit_vf=want_v)

    ffn_w = (row(pre_ffn_w), w_up.astype(F32), w_down.astype(F32), row(post_ffn_w))
    y = _ffn_call(x1, ffn_w, tm=FFN_TILE_M)
    return (y.reshape(n_seq, seq_len, D_MODEL), cstate[:, SUBLANES - (CONV_WIDTH - 1):],
            h_last.reshape(n_seq, SSM_HEADS, SSM_HEAD_DIM, SSM_STATE).astype(h0.dtype),
            vf[0].reshape(n_seq, seq_len, GMLP_WIDTH) if want_v else None)


def kernel(x_prompt, x_sample, cache_conv, state_ssm, pre_mix_w, w_in, gmlp_ln_w, gmlp_ln_b, gmlp_ws, gmlp_bs, conv_w, conv_b, dt_bias, a_log, d_skip, ssm_norm_w, w_branch_a, w_branch_b, w_out, post_mix_w, pre_ffn_w, w_up, w_down, post_ffn_w):
    depth = w_in.shape[0]
    yp, ys = x_prompt, x_sample
    n_p = x_prompt.shape[0]
    hist0 = jnp.zeros((n_p, CONV_WIDTH - 1, CONV_DIM), x_prompt.dtype)
    h00 = jnp.zeros((n_p, SSM_HEADS, SSM_HEAD_DIM, SSM_STATE), state_ssm.dtype)
    conv_p, ssm_p, conv_s, ssm_s, v_s = [], [], [], [], []
    for l in range(depth):
        lw = (pre_mix_w[l], w_in[l], gmlp_ln_w[l], gmlp_ln_b[l], gmlp_ws[l], gmlp_bs[l], conv_w[l], conv_b[l],
              dt_bias[l], a_log[l], d_skip[l], ssm_norm_w[l], w_branch_a[l], w_branch_b[l], w_out[l],
              post_mix_w[l], pre_ffn_w[l], w_up[l], w_down[l], post_ffn_w[l])
        yp, cp, sp, _ = _layer(yp, hist0, h00, seq_len=x_prompt.shape[1], carry=True, want_v=False, lw=lw)
        ys, cs, ss, vs = _layer(ys, cache_conv[l], state_ssm[l], seq_len=x_sample.shape[1], carry=False, want_v=True,
                                lw=lw)
        conv_p.append(cp)
        ssm_p.append(sp)
        conv_s.append(cs)
        ssm_s.append(ss)
        v_s.append(vs)
    return (yp, ys, jnp.stack(conv_p), jnp.stack(ssm_p), jnp.stack(conv_s), jnp.stack(ssm_s), jnp.stack(v_s))
```

```python
import functools

import jax
import jax.numpy as jnp
from jax import lax
from jax.experimental import pallas as pl
from jax.experimental.pallas import tpu as pltpu

D_MODEL = 1024
CHUNK = 64
GMLP_CHUNK = 128
GMLP_WIDTH = 1024
GMLP_GROUPS = 8
GMLP_GROUP_DIM = GMLP_WIDTH // GMLP_GROUPS
SSM_INNER = 2 * D_MODEL
SSM_HEAD_DIM = 64
SSM_HEADS = SSM_INNER // SSM_HEAD_DIM
SSM_GROUPS = 4
SSM_HEADS_PER_GROUP = SSM_HEADS // SSM_GROUPS
SSM_STATE = 128
SSM_GROUP_WIDTH = SSM_INNER // SSM_GROUPS
BC_WIDTH = SSM_GROUPS * SSM_STATE
CONV_WIDTH = 4
CONV_DIM = SSM_INNER + 2 * BC_WIDTH
FFN_HIDDEN = 4 * D_MODEL
EPS = 1e-6
LOG2E = 1.4426950408889634

LANES = 128
SUBLANES = 8
TILE_M = 256
SEG_TILE_M = 128
FFN_TILE_M = 1024
FFN_CHUNK = 1024
SSD_BLOCK = 128
PROJ_CHUNK = 512
VMEM_LIMIT = 60 * 1024 * 1024

BF16 = jnp.bfloat16
F32 = jnp.float32

COL_U = 0
COL_V = COL_U + GMLP_WIDTH
COL_Z = COL_V + GMLP_WIDTH
COL_XBC = COL_Z + SSM_INNER
MAIN_COLS = COL_XBC + CONV_DIM
TAIL_GA = 0
TAIL_GB = TAIL_GA + D_MODEL
TAIL_DT = TAIL_GB + D_MODEL


def _dot(a, b):
    return jnp.dot(a, b, preferred_element_type=F32)


def _dot_nt(a, b):
    return lax.dot_general(a, b, (((1,), (1,)), ((), ())), preferred_element_type=F32)


def _rms_norm(x, w):
    return x * lax.rsqrt(jnp.mean(x * x, axis=-1, keepdims=True) + EPS) * w


def _sigmoid(x):
    return 1.0 / (1.0 + jnp.exp2(x * (-LOG2E)))


def _silu(x):
    return x * _sigmoid(x)


def _gelu_tanh(x):
    c = 0.7978845608028654
    return 0.5 * x * (1.0 + jnp.tanh(c * (x + 0.044715 * (x * x * x))))


def _softplus(x):
    return jnp.maximum(x, 0.0) + jnp.log(1.0 + jnp.exp(-jnp.abs(x)))


def _split3(x):
    hi = x.astype(BF16)
    r1 = x - hi.astype(F32)
    mid = r1.astype(BF16)
    lo = (r1 - mid.astype(F32)).astype(BF16)
    return jnp.concatenate([hi, mid, lo], axis=1)


def _resident(shape):
    nd = len(shape)
    return pl.BlockSpec(shape, lambda i: (0,) * nd, pipeline_mode=pl.Buffered(1))


def _rows(tm, width):
    return pl.BlockSpec((tm, width), lambda i: (i, 0))


def _mix_kernel(n_seg, carry, tiles_per_seq,
                x_ref, hist_ref, h0_ref, pre_w_ref, w_ref, w_tail_ref, ln_w_ref, ln_b_ref, conv_w_ref, conv_b_ref,
                dt_bias_ref, wsp_ref, bsp_ref, w_a_ref, w_b_ref, w_out_ref, nrm_w_ref, post_w_ref, a_ref, dskip_ref,
                y_ref, cstate_ref, hout_ref, *rest):
    tail_ref, state_ref = rest[-2:]
    vf_ref = rest[0] if len(rest) == 3 else None
    tm = x_ref.shape[0]
    seg_len = tm // n_seg
    gw = SSM_GROUP_WIDTH
    cc = PROJ_CHUNK

    if carry:
        first = (pl.program_id(0) % tiles_per_seq) == 0

        @pl.when(first)
        def _():
            for g in range(SSM_GROUPS):
                state_ref[g] = h0_ref[0, g * gw:(g + 1) * gw, :].T

    x = x_ref[...]
    h = _rms_norm(x, pre_w_ref[...]).astype(BF16)

    def proj(col0, c, w=w_ref):
        return _dot(h, w[:, col0 + c * cc:col0 + (c + 1) * cc])

    dt = _softplus(_dot(h, w_tail_ref[:, TAIL_DT:TAIL_DT + LANES]) + dt_bias_ref[...])
    qd = SSD_BLOCK
    qs = min(qd, seg_len)
    blocks_per_seg = seg_len // qs
    seg_shift = seg_len.bit_length() - 1
    row = lax.broadcasted_iota(jnp.int32, (tm, tm), 0)
    col = lax.broadcasted_iota(jnp.int32, (tm, tm), 1)
    causal = (row >= col) & ((row >> seg_shift) == (col >> seg_shift))
    tri = causal[:qd, :qd]
    dta = dt * a_ref[...]
    ones_tri = jnp.where(causal, 1.0, 0.0).astype(BF16)
    acum3 = _dot(ones_tri, _split3(dta))
    acum = acum3[:, :LANES] + acum3[:, LANES:2 * LANES] + acum3[:, 2 * LANES:]
    acum2 = acum * LOG2E
    acum2_t = acum2.T
    half = lax.broadcasted_iota(jnp.int32, (tm, LANES), 1) < SSM_HEAD_DIM
    acum_b = [jnp.broadcast_to(acum2[:, hd:hd + 1], (tm, LANES)) for hd in range(SSM_HEADS)]
    dt_b = [jnp.broadcast_to(dt[:, hd:hd + 1], (tm, LANES)) for hd in range(SSM_HEADS)]
    acum_x = jnp.concatenate([jnp.where(half, acum_b[2 * p], acum_b[2 * p + 1]) for p in range(SSM_HEADS // 2)],
                             axis=1)
    dt_x = jnp.concatenate([jnp.where(half, dt_b[2 * p], dt_b[2 * p + 1]) for p in range(SSM_HEADS // 2)], axis=1)
    e_off, to_end = [], []
    for b in range(tm // qs):
        blk = acum_x[b * qs:(b + 1) * qs]
        if b % blocks_per_seg:
            blk = blk - acum_x[b * qs - 1:b * qs]
        e_off.append(jnp.exp2(blk))
        to_end.append(jnp.exp2(blk[qs - 1:qs] - blk))

    def conv_chunk(c):
        cols = slice(c * cc, (c + 1) * cc)
        raw = proj(COL_XBC, c)
        cw = conv_w_ref[:, cols]
        outs = []
        for s in range(n_seg):
            seg = raw[s * seg_len:(s + 1) * seg_len]
            if carry:
                prev = jnp.where(first, hist_ref[0, :, cols], tail_ref[:, cols])
            else:
                prev = hist_ref[s, :, cols]
            xp = jnp.concatenate([prev, seg], axis=0)
            acc = conv_b_ref[:, cols] + seg * cw[CONV_WIDTH - 1:CONV_WIDTH]
            for k in range(1, CONV_WIDTH):
                shifted = pltpu.roll(xp, k, 0)[SUBLANES:]
                acc = acc + shifted * cw[CONV_WIDTH - 1 - k:CONV_WIDTH - k]
            outs.append(_silu(acc))
            cstate_ref[s, :, cols] = seg[seg_len - SUBLANES:]
        if carry:
            tail_ref[:, cols] = raw[tm - SUBLANES:]
        return outs[0] if n_seg == 1 else jnp.concatenate(outs, axis=0)

    xbc = jnp.concatenate([conv_chunk(c) for c in range(CONV_DIM // cc)], axis=1)
    xs = xbc[:, :SSM_INNER]
    bm = xbc[:, SSM_INNER:SSM_INNER + BC_WIDTH].astype(BF16)
    cm = xbc[:, SSM_INNER + BC_WIDTH:].astype(BF16)
    ga = jnp.concatenate([_sigmoid(proj(TAIL_GA, c, w_tail_ref)) for c in range(D_MODEL // cc)], axis=1)
    gb = jnp.concatenate([_sigmoid(proj(TAIL_GB, c, w_tail_ref)) for c in range(D_MODEL // cc)], axis=1)
    zs = jnp.concatenate([_silu(proj(COL_Z, c)) for c in range(SSM_INNER // cc)], axis=1)
    u = jnp.concatenate([_gelu_tanh(proj(COL_U, c)) for c in range(GMLP_WIDTH // cc)], axis=1)
    v = jnp.concatenate([_gelu_tanh(proj(COL_V, c)) for c in range(GMLP_WIDTH // cc)], axis=1)
    mu = jnp.mean(v, axis=-1, keepdims=True)
    vc = v - mu
    var = jnp.mean(vc * vc, axis=-1, keepdims=True)
    vn = vc * lax.rsqrt(var + EPS) * ln_w_ref[...] + ln_b_ref[...]
    if vf_ref is not None:
        vf_ref[...] = vn

    vb = vn.astype(BF16)
    sp, y_a_cols = [], []
    a_cols = D_MODEL // SSM_GROUPS
    per_step = GMLP_GROUPS // (SSM_GROUPS // 2)

    def branch_a_piece(g):
        if g < SSM_GROUPS // 2:
            sp.extend(_dot(wsp_ref[k], vb[:, k * GMLP_GROUP_DIM:(k + 1) * GMLP_GROUP_DIM])
                      for k in range(per_step * g, per_step * (g + 1)))
            return
        if not y_a_cols:
            y_a_cols.append((u * (jnp.concatenate(sp, axis=1) + bsp_ref[...])).astype(BF16))
        k0 = 2 * (g - SSM_GROUPS // 2)
        y_a_cols.extend(_dot(y_a_cols[0], w_a_ref[:, k * a_cols:(k + 1) * a_cols]) for k in (k0, k0 + 1))

    xdt = xs * dt_x
    xdt_b = xdt.astype(BF16)
    xw = [(xdt[b * qs:(b + 1) * qs] * to_end[b]).astype(BF16) for b in range(tm // qs)]
    low_half = lax.broadcasted_iota(jnp.int32, (qd, LANES), 1) < SSM_HEAD_DIM

    y_groups = []
    for g in range(SSM_GROUPS):
        gcols = slice(g * gw, (g + 1) * gw)
        y_blocks = []
        h_run = None
        for d in range(tm // qd):
            d0, d1 = d * qd, (d + 1) * qd
            cg = cm[d0:d1, g * SSM_STATE:(g + 1) * SSM_STATE]
            bg = bm[d0:d1, g * SSM_STATE:(g + 1) * SSM_STATE]
            cb = jnp.where(tri, _dot_nt(cg, bg), 0.0)
            y_pairs = []
            for j in range(SSM_HEADS_PER_GROUP // 2):
                lhs = []
                for hh in (2 * j, 2 * j + 1):
                    hd = g * SSM_HEADS_PER_GROUP + hh
                    seg = acum_b[hd][d0:d1] - acum2_t[hd:hd + 1, d0:d1]
                    lhs.append((jnp.exp2(jnp.minimum(seg, 0.0)) * cb).astype(BF16))
                c0 = g * gw + j * LANES
                pair = xdt_b[d0:d1, c0:c0 + LANES]
                zero = jnp.zeros_like(pair)
                rhs = jnp.concatenate([jnp.where(low_half, pair, zero), jnp.where(low_half, zero, pair)], axis=0)
                y_pairs.append(_dot(jnp.concatenate(lhs, axis=1), rhs))
            y_diag = jnp.concatenate(y_pairs, axis=1)
            y_off = []
            for b in range(d * (qd // qs), (d + 1) * (qd // qs)):
                r0, r1 = b * qs - d0, (b + 1) * qs - d0
                s = b // blocks_per_seg
                if b % blocks_per_seg == 0:
                    h_run = state_ref[g] if carry else h0_ref[s, gcols, :].T
                eo = e_off[b][:, gcols]
                y_off.append(_dot(cg[r0:r1], h_run.astype(BF16)) * eo)
                bt = bg[r0:r1].astype(F32).T.astype(BF16)
                h_run = h_run * eo[qs - 1:qs] + _dot(bt, xw[b][:, gcols])
                if (b + 1) % blocks_per_seg == 0:
                    if carry:
                        state_ref[g] = h_run
                    else:
                        hout_ref[s, gcols, :] = h_run.T
            y_blocks.append(y_diag + (y_off[0] if len(y_off) == 1 else jnp.concatenate(y_off, axis=0)))
        y_groups.append(y_blocks[0] if len(y_blocks) == 1 else jnp.concatenate(y_blocks, axis=0))
        branch_a_piece(g)
    y_a = jnp.concatenate(y_a_cols[1:], axis=1)
    y = jnp.concatenate(y_groups, axis=1) + dskip_ref[...] * xs

    gz = y * zs
    nw = nrm_w_ref[...]
    o_b = []
    for g in range(SSM_GROUPS):
        blk = gz[:, g * gw:(g + 1) * gw]
        blk = blk * lax.rsqrt(jnp.mean(blk * blk, axis=-1, keepdims=True) + EPS)
        o_b.append((blk * nw[:, g * gw:(g + 1) * gw]).astype(BF16))
    y_b = _dot(jnp.concatenate(o_b, axis=1), w_b_ref[...])

    merged = ga * y_a + gb * y_b
    mixed = _dot(merged.astype(BF16), w_out_ref[...])
    y_ref[...] = x + _rms_norm(mixed, post_w_ref[...])

    if carry:
        @pl.when((pl.program_id(0) % tiles_per_seq) == tiles_per_seq - 1)
        def _():
            for g in range(SSM_GROUPS):
                hout_ref[0, g * gw:(g + 1) * gw, :] = state_ref[g].T


def _mix_call(x, hist, h0, wts, *, tm, n_seg, carry, seq_len, emit_vf):
    t = x.shape[0]
    tiles_per_seq = seq_len // tm if carry else 1
    n_tiles = t // tm
    hp = SSM_HEADS * SSM_HEAD_DIM
    if carry:
        hist_spec = pl.BlockSpec((1, SUBLANES, CONV_DIM), lambda i: (i // tiles_per_seq, 0, 0))
        state_spec = pl.BlockSpec((1, hp, SSM_STATE), lambda i: (i // tiles_per_seq, 0, 0))
    else:
        hist_spec = pl.BlockSpec((n_seg, SUBLANES, CONV_DIM), lambda i: (i, 0, 0))
        state_spec = pl.BlockSpec((n_seg, hp, SSM_STATE), lambda i: (i, 0, 0))
    n_seq = hist.shape[0]
    w_specs = [_resident(w.shape) for w in wts]
    w_specs[1] = _resident((D_MODEL, MAIN_COLS + LANES))
    out_shape = [
        jax.ShapeDtypeStruct((t, D_MODEL), F32),
        jax.ShapeDtypeStruct((n_seq, SUBLANES, CONV_DIM), F32),
        jax.ShapeDtypeStruct(h0.shape, F32),
    ]
    out_specs = [_rows(tm, D_MODEL), hist_spec, state_spec]
    if emit_vf:
        out_shape.append(jax.ShapeDtypeStruct((t, GMLP_WIDTH), F32))
        out_specs.append(_rows(tm, GMLP_WIDTH))
    return pl.pallas_call(
        functools.partial(_mix_kernel, n_seg, carry, tiles_per_seq),
        grid=(n_tiles,),
        in_specs=[_rows(tm, D_MODEL), hist_spec, state_spec] + w_specs,
        out_specs=out_specs,
        out_shape=out_shape,
        scratch_shapes=[pltpu.VMEM((SUBLANES, CONV_DIM), F32),
                        pltpu.VMEM((SSM_GROUPS, SSM_STATE, SSM_GROUP_WIDTH), F32)],
        compiler_params=pltpu.CompilerParams(dimension_semantics=("arbitrary",), vmem_limit_bytes=VMEM_LIMIT),
        name="mix_carry" if carry else "mix_seg",
    )(x, hist, h0, *wts)


def _ffn_kernel(x_ref, pre_w_ref, w_up_ref, w_down_ref, post_w_ref, y_ref):
    x = x_ref[...]
    h = _rms_norm(x, pre_w_ref[...]).astype(BF16)
    acc = None
    for c in range(FFN_HIDDEN // FFN_CHUNK):
        up = _dot(h, w_up_ref[:, c * FFN_CHUNK:(c + 1) * FFN_CHUNK].astype(BF16))
        act = jnp.square(jnp.maximum(up, 0.0)).astype(BF16)
        part = _dot(act, w_down_ref[c * FFN_CHUNK:(c + 1) * FFN_CHUNK, :].astype(BF16))
        acc = part if acc is None else acc + part
    y_ref[...] = x + _rms_norm(acc, post_w_ref[...])


def _ffn_call(x, wts, *, tm):
    t = x.shape[0]
    return pl.pallas_call(
        _ffn_kernel,
        grid=(t // tm,),
        in_specs=[_rows(tm, D_MODEL)] + [_resident(w.shape) for w in wts],
        out_specs=_rows(tm, D_MODEL),
        out_shape=jax.ShapeDtypeStruct((t, D_MODEL), F32),
        compiler_params=pltpu.CompilerParams(dimension_semantics=("arbitrary",), vmem_limit_bytes=VMEM_LIMIT),
        name="ffn",
    )(x, *wts)


def _spatial_tile(ws, bs, n, tm):
    pos = jnp.arange(n)
    mask = (pos[None, :] // CHUNK) <= (pos[:, None] // CHUNK)
    w = jnp.where(mask[None], ws[:, :n, :n], 0.0)
    reps = tm // n
    eye = jnp.eye(reps, dtype=w.dtype)
    tile = jnp.einsum("ab,gij->gaibj", eye, w).reshape(GMLP_GROUPS, tm, tm)
    bias = jnp.repeat(jnp.tile(bs[:, :n].T, (reps, 1)), GMLP_GROUP_DIM, axis=1)
    return tile.astype(BF16), bias.astype(F32)


def _layer(x, hist, h0, *, seq_len, carry, want_v, lw):
    (pre_mix_w, w_in, gmlp_ln_w, gmlp_ln_b, gmlp_ws, gmlp_bs, conv_w, conv_b, dt_bias, a_log, d_skip, ssm_norm_w,
     w_branch_a, w_branch_b, w_out, post_mix_w, pre_ffn_w, w_up, w_down, post_ffn_w) = lw
    n_seq = x.shape[0]
    t = n_seq * seq_len
    tm = TILE_M if carry else SEG_TILE_M
    n_seg = 1 if carry else tm // seq_len
    assert t % tm == 0 and t % FFN_TILE_M == 0 and tm % SSD_BLOCK == 0, (t, tm)
    if carry:
        assert seq_len % tm == 0 and seq_len % GMLP_CHUNK == 0, seq_len
    else:
        assert tm % seq_len == 0 and SUBLANES <= seq_len <= GMLP_CHUNK and seq_len & (seq_len - 1) == 0, seq_len
    xf = x.reshape(t, D_MODEL)
    row = lambda a: a.reshape(1, -1).astype(F32)

    p_dt = MAIN_COLS
    p_ga = p_dt + SSM_HEADS
    wb = w_in.astype(BF16)
    w_tail = jnp.concatenate([w_in[:, p_ga:], w_in[:, p_dt:p_ga],
                              jnp.zeros((D_MODEL, LANES - SSM_HEADS), w_in.dtype)], axis=1).astype(BF16)
    hist8 = jnp.pad(hist.astype(F32), ((0, 0), (SUBLANES - (CONV_WIDTH - 1), 0), (0, 0)))
    h0f = h0.astype(F32).reshape(n_seq, SSM_HEADS * SSM_HEAD_DIM, SSM_STATE)
    wsp, bsp = _spatial_tile(gmlp_ws, gmlp_bs, min(seq_len, GMLP_CHUNK), tm)
    a_pad = jnp.pad(row(-jnp.exp(a_log.astype(F32))), ((0, 0), (0, LANES - SSM_HEADS)))
    dskip_x = jnp.repeat(row(d_skip), SSM_HEAD_DIM, axis=1)
    mix_w = (row(pre_mix_w), wb, w_tail, row(gmlp_ln_w), row(gmlp_ln_b), conv_w.astype(F32), row(conv_b),
             jnp.pad(row(dt_bias), ((0, 0), (0, LANES - SSM_HEADS))),
             wsp, bsp, w_branch_a.astype(BF16), w_branch_b.astype(BF16), w_out.astype(BF16), row(ssm_norm_w),
             row(post_mix_w), a_pad, dskip_x)
    x1, cstate, h_last, *vf = _mix_call(xf, hist8, h0f, mix_w, tm=tm, n_seg=n_seg, carry=carry, seq_len=seq_len,
                                        emit_vf=want_v)

    ffn_w = (row(pre_ffn_w), w_up.astype(F32), w_down.astype(F32), row(post_ffn_w))
    y = _ffn_call(x1, ffn_w, tm=FFN_TILE_M)
    return (y.reshape(n_seq, seq_len, D_MODEL), cstate[:, SUBLANES - (CONV_WIDTH - 1):],
            h_last.reshape(n_seq, SSM_HEADS, SSM_HEAD_DIM, SSM_STATE).astype(h0.dtype),
            vf[0].reshape(n_seq, seq_len, GMLP_WIDTH) if want_v else None)


def kernel(x_prompt, x_sample, cache_conv, state_ssm, pre_mix_w, w_in, gmlp_ln_w, gmlp_ln_b, gmlp_ws, gmlp_bs, conv_w, conv_b, dt_bias, a_log, d_skip, ssm_norm_w, w_branch_a, w_branch_b, w_out, post_mix_w, pre_ffn_w, w_up, w_down, post_ffn_w):
    depth = w_in.shape[0]
    yp, ys = x_prompt, x_sample
    n_p = x_prompt.shape[0]
    hist0 = jnp.zeros((n_p, CONV_WIDTH - 1, CONV_DIM), x_prompt.dtype)
    h00 = jnp.zeros((n_p, SSM_HEADS, SSM_HEAD_DIM, SSM_STATE), state_ssm.dtype)
    conv_p, ssm_p, conv_s, ssm_s, v_s = [], [], [], [], []
    for l in range(depth):
        lw = (pre_mix_w[l], w_in[l], gmlp_ln_w[l], gmlp_ln_b[l], gmlp_ws[l], gmlp_bs[l], conv_w[l], conv_b[l],
              dt_bias[l], a_log[l], d_skip[l], ssm_norm_w[l], w_branch_a[l], w_branch_b[l], w_out[l],
              post_mix_w[l], pre_ffn_w[l], w_up[l], w_down[l], post_ffn_w[l])
        yp, cp, sp, _ = _layer(yp, hist0, h00, seq_len=x_prompt.shape[1], carry=True, want_v=False, lw=lw)
        ys, cs, ss, vs = _layer(ys, cache_conv[l], state_ssm[l], seq_len=x_sample.shape[1], carry=False, want_v=True,
                                lw=lw)
        conv_p.append(cp)
        ssm_p.append(sp)
        conv_s.append(cs)
        ssm_s.append(ss)
        v_s.append(vs)
    return (yp, ys, jnp.stack(conv_p), jnp.stack(ssm_p), jnp.stack(conv_s), jnp.stack(ssm_s), jnp.stack(v_s))
```

```python
import functools

import jax
import jax.numpy as jnp
from jax import lax
from jax.experimental import pallas as pl
from jax.experimental.pallas import tpu as pltpu

D_MODEL = 1024
CHUNK = 64
GMLP_CHUNK = 128
GMLP_WIDTH = 1024
GMLP_GROUPS = 8
GMLP_GROUP_DIM = GMLP_WIDTH // GMLP_GROUPS
SSM_INNER = 2 * D_MODEL
SSM_HEAD_DIM = 64
SSM_HEADS = SSM_INNER // SSM_HEAD_DIM
SSM_GROUPS = 4
SSM_HEADS_PER_GROUP = SSM_HEADS // SSM_GROUPS
SSM_STATE = 128
SSM_GROUP_WIDTH = SSM_INNER // SSM_GROUPS
BC_WIDTH = SSM_GROUPS * SSM_STATE
CONV_WIDTH = 4
CONV_DIM = SSM_INNER + 2 * BC_WIDTH
FFN_HIDDEN = 4 * D_MODEL
EPS = 1e-6
LOG2E = 1.4426950408889634

LANES = 128
SUBLANES = 8
TILE_M = 256
SEG_TILE_M = 128
FFN_TILE_M = 1024
FFN_CHUNK = 1024
SSD_BLOCK = 128
PROJ_CHUNK = 512
VMEM_LIMIT = 60 * 1024 * 1024

BF16 = jnp.bfloat16
F32 = jnp.float32

COL_U = 0
COL_V = COL_U + GMLP_WIDTH
COL_Z = COL_V + GMLP_WIDTH
COL_XBC = COL_Z + SSM_INNER
MAIN_COLS = COL_XBC + CONV_DIM
TAIL_GA = 0
TAIL_GB = TAIL_GA + D_MODEL
TAIL_DT = TAIL_GB + D_MODEL


def _dot(a, b):
    return jnp.dot(a, b, preferred_element_type=F32)


def _dot_nt(a, b):
    return lax.dot_general(a, b, (((1,), (1,)), ((), ())), preferred_element_type=F32)


def _rms_norm(x, w):
    return x * lax.rsqrt(jnp.mean(x * x, axis=-1, keepdims=True) + EPS) * w


def _sigmoid(x):
    return 1.0 / (1.0 + jnp.exp2(x * (-LOG2E)))


def _silu(x):
    return x * _sigmoid(x)


def _gelu_tanh(x):
    c = 0.7978845608028654
    return 0.5 * x * (1.0 + jnp.tanh(c * (x + 0.044715 * (x * x * x))))


def _softplus(x):
    return jnp.maximum(x, 0.0) + jnp.log(1.0 + jnp.exp(-jnp.abs(x)))


def _split3(x):
    hi = x.astype(BF16)
    r1 = x - hi.astype(F32)
    mid = r1.astype(BF16)
    lo = (r1 - mid.astype(F32)).astype(BF16)
    return jnp.concatenate([hi, mid, lo], axis=1)


def _resident(shape):
    nd = len(shape)
    return pl.BlockSpec(shape, lambda i: (0,) * nd, pipeline_mode=pl.Buffered(1))


def _rows(tm, width):
    return pl.BlockSpec((tm, width), lambda i: (i, 0))


def _mix_kernel(n_seg, carry, tiles_per_seq,
                x_ref, hist_ref, h0_ref, pre_w_ref, w_ref, w_tail_ref, ln_w_ref, ln_b_ref, conv_w_ref, conv_b_ref,
                dt_bias_ref, wsp_ref, bsp_ref, w_a_ref, w_b_ref, w_out_ref, nrm_w_ref, post_w_ref, a_ref, dskip_ref,
                y_ref, cstate_ref, hout_ref, *rest):
    tail_ref, state_ref = rest[-2:]
    vf_ref = rest[0] if len(rest) == 3 else None
    tm = x_ref.shape[0]
    seg_len = tm // n_seg
    gw = SSM_GROUP_WIDTH
    cc = PROJ_CHUNK

    if carry:
        first = (pl.program_id(0) % tiles_per_seq) == 0

        @pl.when(first)
        def _():
            for g in range(SSM_GROUPS):
                state_ref[g] = h0_ref[0, g * gw:(g + 1) * gw, :].T

    x = x_ref[...]
    h = _rms_norm(x, pre_w_ref[...]).astype(BF16)

    def proj(col0, c, w=w_ref):
        return _dot(h, w[:, col0 + c * cc:col0 + (c + 1) * cc])

    dt = _softplus(_dot(h, w_tail_ref[:, TAIL_DT:TAIL_DT + LANES]) + dt_bias_ref[...])
    qd = SSD_BLOCK
    qs = min(qd, seg_len)
    blocks_per_seg = seg_len // qs
    seg_shift = seg_len.bit_length() - 1
    row = lax.broadcasted_iota(jnp.int32, (tm, tm), 0)
    col = lax.broadcasted_iota(jnp.int32, (tm, tm), 1)
    causal = (row >= col) & ((row >> seg_shift) == (col >> seg_shift))
    tri = causal[:qd, :qd]
    dta = dt * a_ref[...]
    ones_tri = jnp.where(causal, 1.0, 0.0).astype(BF16)
    acum3 = _dot(ones_tri, _split3(dta))
    acum = acum3[:, :LANES] + acum3[:, LANES:2 * LANES] + acum3[:, 2 * LANES:]
    acum2 = acum * LOG2E
    acum2_t = acum2.T
    half = lax.broadcasted_iota(jnp.int32, (tm, LANES), 1) < SSM_HEAD_DIM
    acum_b = [jnp.broadcast_to(acum2[:, hd:hd + 1], (tm, LANES)) for hd in range(SSM_HEADS)]
    dt_b = [jnp.broadcast_to(dt[:, hd:hd + 1], (tm, LANES)) for hd in range(SSM_HEADS)]
    acum_x = jnp.concatenate([jnp.where(half, acum_b[2 * p], acum_b[2 * p + 1]) for p in range(SSM_HEADS // 2)],
                             axis=1)
    dt_x = jnp.concatenate([jnp.where(half, dt_b[2 * p], dt_b[2 * p + 1]) for p in range(SSM_HEADS // 2)], axis=1)
    e_off, to_end = [], []
    for b in range(tm // qs):
        blk = acum_x[b * qs:(b + 1) * qs]
        if b % blocks_per_seg:
            blk = blk - acum_x[b * qs - 1:b * qs]
        e_off.append(jnp.exp2(blk))
        to_end.append(jnp.exp2(blk[qs - 1:qs] - blk))

    def conv_chunk(c):
        cols = slice(c * cc, (c + 1) * cc)
        raw = proj(COL_XBC, c)
        cw = conv_w_ref[:, cols]
        outs = []
        for s in range(n_seg):
            seg = raw[s * seg_len:(s + 1) * seg_len]
            if carry:
                prev = jnp.where(first, hist_ref[0, :, cols], tail_ref[:, cols])
            else:
                prev = hist_ref[s, :, cols]
            xp = jnp.concatenate([prev, seg], axis=0)
            acc = conv_b_ref[:, cols] + seg * cw[CONV_WIDTH - 1:CONV_WIDTH]
            for k in range(1, CONV_WIDTH):
                shifted = pltpu.roll(xp, k, 0)[SUBLANES:]
                acc = acc + shifted * cw[CONV_WIDTH - 1 - k:CONV_WIDTH - k]
            outs.append(_silu(acc))
            cstate_ref[s, :, cols] = seg[seg_len - SUBLANES:]
        if carry:
            tail_ref[:, cols] = raw[tm - SUBLANES:]
        return outs[0] if n_seg == 1 else jnp.concatenate(outs, axis=0)

    xbc = jnp.concatenate([conv_chunk(c) for c in range(CONV_DIM // cc)], axis=1)
    xs = xbc[:, :SSM_INNER]
    bm = xbc[:, SSM_INNER:SSM_INNER + BC_WIDTH].astype(BF16)
    cm = xbc[:, SSM_INNER + BC_WIDTH:].astype(BF16)
    v = jnp.concatenate([_gelu_tanh(proj(COL_V, c)) for c in range(GMLP_WIDTH // cc)], axis=1)
    mu = jnp.mean(v, axis=-1, keepdims=True)
    vc = v - mu
    var = jnp.mean(vc * vc, axis=-1, keepdims=True)
    vn = vc * lax.rsqrt(var + EPS) * ln_w_ref[...] + ln_b_ref[...]
    if vf_ref is not None:
        vf_ref[...] = vn
    u = jnp.concatenate([_gelu_tanh(proj(COL_U, c)) for c in range(GMLP_WIDTH // cc)], axis=1)
    zs = jnp.concatenate([_silu(proj(COL_Z, c)) for c in range(SSM_INNER // cc)], axis=1)
    ga = jnp.concatenate([_sigmoid(proj(TAIL_GA, c, w_tail_ref)) for c in range(D_MODEL // cc)], axis=1)
    gb = jnp.concatenate([_sigmoid(proj(TAIL_GB, c, w_tail_ref)) for c in range(D_MODEL // cc)], axis=1)

    vb = vn.astype(BF16)
    sp, y_a_cols = [], []
    a_cols = D_MODEL // SSM_GROUPS
    per_step = GMLP_GROUPS // (SSM_GROUPS // 2)

    def branch_a_piece(g):
        if g < SSM_GROUPS // 2:
            sp.extend(_dot(wsp_ref[k], vb[:, k * GMLP_GROUP_DIM:(k + 1) * GMLP_GROUP_DIM])
                      for k in range(per_step * g, per_step * (g + 1)))
            return
        if not y_a_cols:
            y_a_cols.append((u * (jnp.concatenate(sp, axis=1) + bsp_ref[...])).astype(BF16))
        k0 = 2 * (g - SSM_GROUPS // 2)
        y_a_cols.extend(_dot(y_a_cols[0], w_a_ref[:, k * a_cols:(k + 1) * a_cols]) for k in (k0, k0 + 1))

    xdt = xs * dt_x
    xdt_b = xdt.astype(BF16)
    xw = [(xdt[b * qs:(b + 1) * qs] * to_end[b]).astype(BF16) for b in range(tm // qs)]
    low_half = lax.broadcasted_iota(jnp.int32, (qd, LANES), 1) < SSM_HEAD_DIM

    y_groups = []
    for g in range(SSM_GROUPS):
        gcols = slice(g * gw, (g + 1) * gw)
        y_blocks = []
        h_run = None
        for d in range(tm // qd):
            d0, d1 = d * qd, (d + 1) * qd
            cg = cm[d0:d1, g * SSM_STATE:(g + 1) * SSM_STATE]
            bg = bm[d0:d1, g * SSM_STATE:(g + 1) * SSM_STATE]
            cb = jnp.where(tri, _dot_nt(cg, bg), 0.0)
            y_pairs = []
            for j in range(SSM_HEADS_PER_GROUP // 2):
                lhs = []
                for hh in (2 * j, 2 * j + 1):
                    hd = g * SSM_HEADS_PER_GROUP + hh
                    seg = acum_b[hd][d0:d1] - acum2_t[hd:hd + 1, d0:d1]
                    lhs.append((jnp.exp2(jnp.minimum(seg, 0.0)) * cb).astype(BF16))
                c0 = g * gw + j * LANES
                pair = xdt_b[d0:d1, c0:c0 + LANES]
                zero = jnp.zeros_like(pair)
                rhs = jnp.concatenate([jnp.where(low_half, pair, zero), jnp.where(low_half, zero, pair)], axis=0)
                y_pairs.append(_dot(jnp.concatenate(lhs, axis=1), rhs))
            y_diag = jnp.concatenate(y_pairs, axis=1)
            y_off = []
            for b in range(d * (qd // qs), (d + 1) * (qd // qs)):
                r0, r1 = b * qs - d0, (b + 1) * qs - d0
                s = b // blocks_per_seg
                if b % blocks_per_seg == 0:
                    h_run = state_ref[g] if carry else h0_ref[s, gcols, :].T
                eo = e_off[b][:, gcols]
                y_off.append(_dot(cg[r0:r1], h_run.astype(BF16)) * eo)
                bt = bg[r0:r1].astype(F32).T.astype(BF16)
                h_run = h_run * eo[qs - 1:qs] + _dot(bt, xw[b][:, gcols])
                if (b + 1) % blocks_per_seg == 0:
                    if carry:
                        state_ref[g] = h_run
                    else:
                        hout_ref[s, gcols, :] = h_run.T
            y_blocks.append(y_diag + (y_off[0] if len(y_off) == 1 else jnp.concatenate(y_off, axis=0)))
        y_groups.append(y_blocks[0] if len(y_blocks) == 1 else jnp.concatenate(y_blocks, axis=0))
        branch_a_piece(g)
    y_a = jnp.concatenate(y_a_cols[1:], axis=1)
    y = jnp.concatenate(y_groups, axis=1) + dskip_ref[...] * xs

    gz = y * zs
    nw = nrm_w_ref[...]
    o_b = []
    for g in range(SSM_GROUPS):
        blk = gz[:, g * gw:(g + 1) * gw]
        blk = blk * lax.rsqrt(jnp.mean(blk * blk, axis=-1, keepdims=True) + EPS)
        o_b.append((blk * nw[:, g * gw:(g + 1) * gw]).astype(BF16))
    y_b = _dot(jnp.concatenate(o_b, axis=1), w_b_ref[...])

    merged = ga * y_a + gb * y_b
    mixed = _dot(merged.astype(BF16), w_out_ref[...])
    y_ref[...] = x + _rms_norm(mixed, post_w_ref[...])

    if carry:
        @pl.when((pl.program_id(0) % tiles_per_seq) == tiles_per_seq - 1)
        def _():
            for g in range(SSM_GROUPS):
                hout_ref[0, g * gw:(g + 1) * gw, :] = state_ref[g].T


def _mix_call(x, hist, h0, wts, *, tm, n_seg, carry, seq_len, emit_vf):
    t = x.shape[0]
    tiles_per_seq = seq_len // tm if carry else 1
    n_tiles = t // tm
    hp = SSM_HEADS * SSM_HEAD_DIM
    if carry:
        hist_spec = pl.BlockSpec((1, SUBLANES, CONV_DIM), lambda i: (i // tiles_per_seq, 0, 0))
        state_spec = pl.BlockSpec((1, hp, SSM_STATE), lambda i: (i // tiles_per_seq, 0, 0))
    else:
        hist_spec = pl.BlockSpec((n_seg, SUBLANES, CONV_DIM), lambda i: (i, 0, 0))
        state_spec = pl.BlockSpec((n_seg, hp, SSM_STATE), lambda i: (i, 0, 0))
    n_seq = hist.shape[0]
    w_specs = [_resident(w.shape) for w in wts]
    w_specs[1] = _resident((D_MODEL, MAIN_COLS + LANES))
    out_shape = [
        jax.ShapeDtypeStruct((t, D_MODEL), F32),
        jax.ShapeDtypeStruct((n_seq, SUBLANES, CONV_DIM), F32),
        jax.ShapeDtypeStruct(h0.shape, F32),
    ]
    out_specs = [_rows(tm, D_MODEL), hist_spec, state_spec]
    if emit_vf:
        out_shape.append(jax.ShapeDtypeStruct((t, GMLP_WIDTH), F32))
        out_specs.append(_rows(tm, GMLP_WIDTH))
    return pl.pallas_call(
        functools.partial(_mix_kernel, n_seg, carry, tiles_per_seq),
        grid=(n_tiles,),
        in_specs=[_rows(tm, D_MODEL), hist_spec, state_spec] + w_specs,
        out_specs=out_specs,
        out_shape=out_shape,
        scratch_shapes=[pltpu.VMEM((SUBLANES, CONV_DIM), F32),
                        pltpu.VMEM((SSM_GROUPS, SSM_STATE, SSM_GROUP_WIDTH), F32)],
        compiler_params=pltpu.CompilerParams(dimension_semantics=("arbitrary",), vmem_limit_bytes=VMEM_LIMIT),
        name="mix_carry" if carry else "mix_seg",
    )(x, hist, h0, *wts)


def _ffn_kernel(x_ref, pre_w_ref, w_up_ref, w_down_ref, post_w_ref, y_ref):
    x = x_ref[...]
    h = _rms_norm(x, pre_w_ref[...]).astype(BF16)
    acc = None
    for c in range(FFN_HIDDEN // FFN_CHUNK):
        up = _dot(h, w_up_ref[:, c * FFN_CHUNK:(c + 1) * FFN_CHUNK].astype(BF16))
        act = jnp.square(jnp.maximum(up, 0.0)).astype(BF16)
        part = _dot(act, w_down_ref[c * FFN_CHUNK:(c + 1) * FFN_CHUNK, :].astype(BF16))
        acc = part if acc is None else acc + part
    y_ref[...] = x + _rms_norm(acc, post_w_ref[...])


def _ffn_call(x, wts, *, tm):
    t = x.shape[0]
    return pl.pallas_call(
        _ffn_kernel,
        grid=(t // tm,),
        in_specs=[_rows(tm, D_MODEL)] + [_resident(w.shape) for w in wts],
        out_specs=_rows(tm, D_MODEL),
        out_shape=jax.ShapeDtypeStruct((t, D_MODEL), F32),
        compiler_params=pltpu.CompilerParams(dimension_semantics=("arbitrary",), vmem_limit_bytes=VMEM_LIMIT),
        name="ffn",
    )(x, *wts)


def _spatial_tile(ws, bs, n, tm):
    pos = jnp.arange(n)
    mask = (pos[None, :] // CHUNK) <= (pos[:, None] // CHUNK)
    w = jnp.where(mask[None], ws[:, :n, :n], 0.0)
    reps = tm // n
    eye = jnp.eye(reps, dtype=w.dtype)
    tile = jnp.einsum("ab,gij->gaibj", eye, w).reshape(GMLP_GROUPS, tm, tm)
    bias = jnp.repeat(jnp.tile(bs[:, :n].T, (reps, 1)), GMLP_GROUP_DIM, axis=1)
    return tile.astype(BF16), bias.astype(F32)


def _layer(x, hist, h0, *, seq_len, carry, want_v, lw):
    (pre_mix_w, w_in, gmlp_ln_w, gmlp_ln_b, gmlp_ws, gmlp_bs, conv_w, conv_b, dt_bias, a_log, d_skip, ssm_norm_w,
     w_branch_a, w_branch_b, w_out, post_mix_w, pre_ffn_w, w_up, w_down, post_ffn_w) = lw
    n_seq = x.shape[0]
    t = n_seq * seq_len
    tm = TILE_M if carry else SEG_TILE_M
    n_seg = 1 if carry else tm // seq_len
    assert t % tm == 0 and t % FFN_TILE_M == 0 and tm % SSD_BLOCK == 0, (t, tm)
    if carry:
        assert seq_len % tm == 0 and seq_len % GMLP_CHUNK == 0, seq_len
    else:
        assert tm % seq_len == 0 and SUBLANES <= seq_len <= GMLP_CHUNK and seq_len & (seq_len - 1) == 0, seq_len
    xf = x.reshape(t, D_MODEL)
    row = lambda a: a.reshape(1, -1).astype(F32)

    p_dt = MAIN_COLS
    p_ga = p_dt + SSM_HEADS
    wb = w_in.astype(BF16)
    w_tail = jnp.concatenate([w_in[:, p_ga:], w_in[:, p_dt:p_ga],
                              jnp.zeros((D_MODEL, LANES - SSM_HEADS), w_in.dtype)], axis=1).astype(BF16)
    hist8 = jnp.pad(hist.astype(F32), ((0, 0), (SUBLANES - (CONV_WIDTH - 1), 0), (0, 0)))
    h0f = h0.astype(F32).reshape(n_seq, SSM_HEADS * SSM_HEAD_DIM, SSM_STATE)
    wsp, bsp = _spatial_tile(gmlp_ws, gmlp_bs, min(seq_len, GMLP_CHUNK), tm)
    a_pad = jnp.pad(row(-jnp.exp(a_log.astype(F32))), ((0, 0), (0, LANES - SSM_HEADS)))
    dskip_x = jnp.repeat(row(d_skip), SSM_HEAD_DIM, axis=1)
    mix_w = (row(pre_mix_w), wb, w_tail, row(gmlp_ln_w), row(gmlp_ln_b), conv_w.astype(F32), row(conv_b),
             jnp.pad(row(dt_bias), ((0, 0), (0, LANES - SSM_HEADS))),
             wsp, bsp, w_branch_a.astype(BF16), w_branch_b.astype(BF16), w_out.astype(BF16), row(ssm_norm_w),
             row(post_mix_w), a_pad, dskip_x)
    x1, cstate, h_last, *vf = _mix_call(xf, hist8, h0f, mix_w, tm=tm, n_seg=n_seg, carry=carry, seq_len=seq_len,
                                        emit_vf=want_v)

    ffn_w = (row(pre_ffn_w), w_up.astype(F32), w_down.astype(F32), row(post_ffn_w))
    y = _ffn_call(x1, ffn_w, tm=FFN_TILE_M)
    return (y.reshape(n_seq, seq_len, D_MODEL), cstate[:, SUBLANES - (CONV_WIDTH - 1):],
            h_last.reshape(n_seq, SSM_HEADS, SSM_HEAD_DIM, SSM_STATE).astype(h0.dtype),
            vf[0].reshape(n_seq, seq_len, GMLP_WIDTH) if want_v else None)


def kernel(x_prompt, x_sample, cache_conv, state_ssm, pre_mix_w, w_in, gmlp_ln_w, gmlp_ln_b, gmlp_ws, gmlp_bs, conv_w, conv_b, dt_bias, a_log, d_skip, ssm_norm_w, w_branch_a, w_branch_b, w_out, post_mix_w, pre_ffn_w, w_up, w_down, post_ffn_w):
    depth = w_in.shape[0]
    yp, ys = x_prompt, x_sample
    n_p = x_prompt.shape[0]
    hist0 = jnp.zeros((n_p, CONV_WIDTH - 1, CONV_DIM), x_prompt.dtype)
    h00 = jnp.zeros((n_p, SSM_HEADS, SSM_HEAD_DIM, SSM_STATE), state_ssm.dtype)
    conv_p, ssm_p, conv_s, ssm_s, v_s = [], [], [], [], []
    for l in range(depth):
        lw = (pre_mix_w[l], w_in[l], gmlp_ln_w[l], gmlp_ln_b[l], gmlp_ws[l], gmlp_bs[l], conv_w[l], conv_b[l],
              dt_bias[l], a_log[l], d_skip[l], ssm_norm_w[l], w_branch_a[l], w_branch_b[l], w_out[l],
              post_mix_w[l], pre_ffn_w[l], w_up[l], w_down[l], post_ffn_w[l])
        yp, cp, sp, _ = _layer(yp, hist0, h00, seq_len=x_prompt.shape[1], carry=True, want_v=False, lw=lw)
        ys, cs, ss, vs = _layer(ys, cache_conv[l], state_ssm[l], seq_len=x_sample.shape[1], carry=False, want_v=True,
                                lw=lw)
        conv_p.append(cp)
        ssm_p.append(sp)
        conv_s.append(cs)
        ssm_s.append(ss)
        v_s.append(vs)
    return (yp, ys, jnp.stack(conv_p), jnp.stack(ssm_p), jnp.stack(conv_s), jnp.stack(ssm_s), jnp.stack(v_s))
```

```python
import functools

import jax
import jax.numpy as jnp
from jax import lax
from jax.experimental import pallas as pl
from jax.experimental.pallas import tpu as pltpu

D_MODEL = 1024
CHUNK = 64
GMLP_CHUNK = 128
GMLP_WIDTH = 1024
GMLP_GROUPS = 8
GMLP_GROUP_DIM = GMLP_WIDTH // GMLP_GROUPS
SSM_INNER = 2 * D_MODEL
SSM_HEAD_DIM = 64
SSM_HEADS = SSM_INNER // SSM_HEAD_DIM
SSM_GROUPS = 4
SSM_HEADS_PER_GROUP = SSM_HEADS // SSM_GROUPS
SSM_STATE = 128
SSM_GROUP_WIDTH = SSM_INNER // SSM_GROUPS
BC_WIDTH = SSM_GROUPS * SSM_STATE
CONV_WIDTH = 4
CONV_DIM = SSM_INNER + 2 * BC_WIDTH
FFN_HIDDEN = 4 * D_MODEL
EPS = 1e-6
LOG2E = 1.4426950408889634

LANES = 128
SUBLANES = 8
TILE_M = 256
SEG_TILE_M = 128
FFN_TILE_M = 1024
FFN_CHUNK = 1024
SSD_BLOCK = 128
PROJ_CHUNK = 512
VMEM_LIMIT = 60 * 1024 * 1024

BF16 = jnp.bfloat16
F32 = jnp.float32

COL_U = 0
COL_V = COL_U + GMLP_WIDTH
COL_Z = COL_V + GMLP_WIDTH
COL_XBC = COL_Z + SSM_INNER
MAIN_COLS = COL_XBC + CONV_DIM
TAIL_GA = 0
TAIL_GB = TAIL_GA + D_MODEL
TAIL_DT = TAIL_GB + D_MODEL


def _dot(a, b):
    return jnp.dot(a, b, preferred_element_type=F32)


def _dot_nt(a, b):
    return lax.dot_general(a, b, (((1,), (1,)), ((), ())), preferred_element_type=F32)


def _rms_norm(x, w):
    return x * lax.rsqrt(jnp.mean(x * x, axis=-1, keepdims=True) + EPS) * w


def _sigmoid(x):
    return 1.0 / (1.0 + jnp.exp2(x * (-LOG2E)))


def _silu(x):
    return x * _sigmoid(x)


def _gelu_tanh(x):
    c = 0.7978845608028654
    return 0.5 * x * (1.0 + jnp.tanh(c * (x + 0.044715 * (x * x * x))))


def _softplus(x):
    return jnp.maximum(x, 0.0) + jnp.log(1.0 + jnp.exp(-jnp.abs(x)))


def _resident(shape):
    nd = len(shape)
    return pl.BlockSpec(shape, lambda i: (0,) * nd, pipeline_mode=pl.Buffered(1))


def _rows(tm, width):
    return pl.BlockSpec((tm, width), lambda i: (i, 0))


def _mix_kernel(n_seg, carry, tiles_per_seq,
                x_ref, hist_ref, h0_ref, pre_w_ref, w_ref, w_tail_ref, ln_w_ref, ln_b_ref, conv_w_ref, conv_b_ref,
                dt_bias_ref, wsp_ref, bsp_ref, w_a_ref, w_b_ref, w_out_ref, nrm_w_ref, post_w_ref, a_ref, dskip_ref,
                y_ref, cstate_ref, hout_ref, *rest):
    tail_ref, state_ref = rest[-2:]
    vf_ref = rest[0] if len(rest) == 3 else None
    tm = x_ref.shape[0]
    seg_len = tm // n_seg
    gw = SSM_GROUP_WIDTH
    cc = PROJ_CHUNK

    if carry:
        first = (pl.program_id(0) % tiles_per_seq) == 0

        @pl.when(first)
        def _():
            for g in range(SSM_GROUPS):
                state_ref[g] = h0_ref[0, g * gw:(g + 1) * gw, :].T

    x = x_ref[...]
    h = _rms_norm(x, pre_w_ref[...]).astype(BF16)

    def proj(col0, c, w=w_ref):
        return _dot(h, w[:, col0 + c * cc:col0 + (c + 1) * cc])

    dt = _softplus(_dot(h, w_tail_ref[:, TAIL_DT:TAIL_DT + LANES]) + dt_bias_ref[...])
    qd = SSD_BLOCK
    qs = min(qd, seg_len)
    blocks_per_seg = seg_len // qs
    seg_shift = seg_len.bit_length() - 1
    rq = lax.broadcasted_iota(jnp.int32, (qd, qd), 0)
    cq = lax.broadcasted_iota(jnp.int32, (qd, qd), 1)
    tri = (rq >= cq) & ((rq >> seg_shift) == (cq >> seg_shift))
    row_in_seg = lax.broadcasted_iota(jnp.int32, (tm, LANES), 0) & (seg_len - 1)
    acum = dt * a_ref[...]
    k = 1
    while k < seg_len:
        acum = acum + jnp.where(row_in_seg >= k, pltpu.roll(acum, k, 0), 0.0)
        k *= 2
    acum2 = acum * LOG2E
    acum2_t = acum2.T
    half = lax.broadcasted_iota(jnp.int32, (tm, LANES), 1) < SSM_HEAD_DIM
    acum_b = [jnp.broadcast_to(acum2[:, hd:hd + 1], (tm, LANES)) for hd in range(SSM_HEADS)]
    dt_b = [jnp.broadcast_to(dt[:, hd:hd + 1], (tm, LANES)) for hd in range(SSM_HEADS)]
    acum_x = jnp.concatenate([jnp.where(half, acum_b[2 * p], acum_b[2 * p + 1]) for p in range(SSM_HEADS // 2)],
                             axis=1)
    dt_x = jnp.concatenate([jnp.where(half, dt_b[2 * p], dt_b[2 * p + 1]) for p in range(SSM_HEADS // 2)], axis=1)
    e_off, to_end = [], []
    for b in range(tm // qs):
        blk = acum_x[b * qs:(b + 1) * qs]
        if b % blocks_per_seg:
            blk = blk - acum_x[b * qs - 1:b * qs]
        e_off.append(jnp.exp2(blk))
        to_end.append(jnp.exp2(blk[qs - 1:qs] - blk))

    def conv_chunk(c):
        cols = slice(c * cc, (c + 1) * cc)
        raw = proj(COL_XBC, c)
        cw = conv_w_ref[:, cols]
        outs = []
        for s in range(n_seg):
            seg = raw[s * seg_len:(s + 1) * seg_len]
            if carry:
                prev = jnp.where(first, hist_ref[0, :, cols], tail_ref[:, cols])
            else:
                prev = hist_ref[s, :, cols]
            xp = jnp.concatenate([prev, seg], axis=0)
            acc = conv_b_ref[:, cols] + seg * cw[CONV_WIDTH - 1:CONV_WIDTH]
            for k in range(1, CONV_WIDTH):
                shifted = pltpu.roll(xp, k, 0)[SUBLANES:]
                acc = acc + shifted * cw[CONV_WIDTH - 1 - k:CONV_WIDTH - k]
            outs.append(_silu(acc))
            cstate_ref[s, :, cols] = seg[seg_len - SUBLANES:]
        if carry:
            tail_ref[:, cols] = raw[tm - SUBLANES:]
        return outs[0] if n_seg == 1 else jnp.concatenate(outs, axis=0)

    xbc = jnp.concatenate([conv_chunk(c) for c in range(CONV_DIM // cc)], axis=1)
    xs = xbc[:, :SSM_INNER]
    bm = xbc[:, SSM_INNER:SSM_INNER + BC_WIDTH].astype(BF16)
    cm = xbc[:, SSM_INNER + BC_WIDTH:].astype(BF16)
    ga = jnp.concatenate([_sigmoid(proj(TAIL_GA, c, w_tail_ref)) for c in range(D_MODEL // cc)], axis=1)
    gb = jnp.concatenate([_sigmoid(proj(TAIL_GB, c, w_tail_ref)) for c in range(D_MODEL // cc)], axis=1)
    zs = jnp.concatenate([_silu(proj(COL_Z, c)) for c in range(SSM_INNER // cc)], axis=1)
    u = jnp.concatenate([_gelu_tanh(proj(COL_U, c)) for c in range(GMLP_WIDTH // cc)], axis=1)
    v = jnp.concatenate([_gelu_tanh(proj(COL_V, c)) for c in range(GMLP_WIDTH // cc)], axis=1)
    mu = jnp.mean(v, axis=-1, keepdims=True)
    vc = v - mu
    var = jnp.mean(vc * vc, axis=-1, keepdims=True)
    vn = vc * lax.rsqrt(var + EPS) * ln_w_ref[...] + ln_b_ref[...]
    if vf_ref is not None:
        vf_ref[...] = vn

    vb = vn.astype(BF16)
    sp, y_a_cols = [], []
    a_cols = D_MODEL // SSM_GROUPS
    per_step = GMLP_GROUPS // (SSM_GROUPS // 2)

    def branch_a_piece(g):
        if g < SSM_GROUPS // 2:
            sp.extend(_dot(wsp_ref[k], vb[:, k * GMLP_GROUP_DIM:(k + 1) * GMLP_GROUP_DIM])
                      for k in range(per_step * g, per_step * (g + 1)))
            return
        if not y_a_cols:
            y_a_cols.append((u * (jnp.concatenate(sp, axis=1) + bsp_ref[...])).astype(BF16))
        k0 = 2 * (g - SSM_GROUPS // 2)
        y_a_cols.extend(_dot(y_a_cols[0], w_a_ref[:, k * a_cols:(k + 1) * a_cols]) for k in (k0, k0 + 1))

    xdt = xs * dt_x
    xdt_b = xdt.astype(BF16)
    xw = [(xdt[b * qs:(b + 1) * qs] * to_end[b]).astype(BF16) for b in range(tm // qs)]
    low_half = lax.broadcasted_iota(jnp.int32, (qd, LANES), 1) < SSM_HEAD_DIM

    y_groups = []
    for g in range(SSM_GROUPS):
        gcols = slice(g * gw, (g + 1) * gw)
        y_blocks = []
        h_run = None
        for d in range(tm // qd):
            d0, d1 = d * qd, (d + 1) * qd
            cg = cm[d0:d1, g * SSM_STATE:(g + 1) * SSM_STATE]
            bg = bm[d0:d1, g * SSM_STATE:(g + 1) * SSM_STATE]
            cb = jnp.where(tri, _dot_nt(cg, bg), 0.0)
            y_pairs = []
            for j in range(SSM_HEADS_PER_GROUP // 2):
                lhs = []
                for hh in (2 * j, 2 * j + 1):
                    hd = g * SSM_HEADS_PER_GROUP + hh
                    seg = acum_b[hd][d0:d1] - acum2_t[hd:hd + 1, d0:d1]
                    lhs.append((jnp.exp2(jnp.minimum(seg, 0.0)) * cb).astype(BF16))
                c0 = g * gw + j * LANES
                pair = xdt_b[d0:d1, c0:c0 + LANES]
                zero = jnp.zeros_like(pair)
                rhs = jnp.concatenate([jnp.where(low_half, pair, zero), jnp.where(low_half, zero, pair)], axis=0)
                y_pairs.append(_dot(jnp.concatenate(lhs, axis=1), rhs))
            y_diag = jnp.concatenate(y_pairs, axis=1)
            y_off = []
            for b in range(d * (qd // qs), (d + 1) * (qd // qs)):
                r0, r1 = b * qs - d0, (b + 1) * qs - d0
                s = b // blocks_per_seg
                if b % blocks_per_seg == 0:
                    h_run = state_ref[g] if carry else h0_ref[s, gcols, :].T
                eo = e_off[b][:, gcols]
                y_off.append(_dot(cg[r0:r1], h_run.astype(BF16)) * eo)
                bt = bg[r0:r1].astype(F32).T.astype(BF16)
                h_run = h_run * eo[qs - 1:qs] + _dot(bt, xw[b][:, gcols])
                if (b + 1) % blocks_per_seg == 0:
                    if carry:
                        state_ref[g] = h_run
                    else:
                        hout_ref[s, gcols, :] = h_run.T
            y_blocks.append(y_diag + (y_off[0] if len(y_off) == 1 else jnp.concatenate(y_off, axis=0)))
        y_groups.append(y_blocks[0] if len(y_blocks) == 1 else jnp.concatenate(y_blocks, axis=0))
        branch_a_piece(g)
    y_a = jnp.concatenate(y_a_cols[1:], axis=1)
    y = jnp.concatenate(y_groups, axis=1) + dskip_ref[...] * xs

    gz = y * zs
    nw = nrm_w_ref[...]
    o_b = []
    for g in range(SSM_GROUPS):
        blk = gz[:, g * gw:(g + 1) * gw]
        blk = blk * lax.rsqrt(jnp.mean(blk * blk, axis=-1, keepdims=True) + EPS)
        o_b.append((blk * nw[:, g * gw:(g + 1) * gw]).astype(BF16))
    y_b = _dot(jnp.concatenate(o_b, axis=1), w_b_ref[...])

    merged = ga * y_a + gb * y_b
    mixed = _dot(merged.astype(BF16), w_out_ref[...])
    y_ref[...] = x + _rms_norm(mixed, post_w_ref[...])

    if carry:
        @pl.when((pl.program_id(0) % tiles_per_seq) == tiles_per_seq - 1)
        def _():
            for g in range(SSM_GROUPS):
                hout_ref[0, g * gw:(g + 1) * gw, :] = state_ref[g].T


def _mix_call(x, hist, h0, wts, *, tm, n_seg, carry, seq_len, emit_vf):
    t = x.shape[0]
    tiles_per_seq = seq_len // tm if carry else 1
    n_tiles = t // tm
    hp = SSM_HEADS * SSM_HEAD_DIM
    if carry:
        hist_spec = pl.BlockSpec((1, SUBLANES, CONV_DIM), lambda i: (i // tiles_per_seq, 0, 0))
        state_spec = pl.BlockSpec((1, hp, SSM_STATE), lambda i: (i // tiles_per_seq, 0, 0))
    else:
        hist_spec = pl.BlockSpec((n_seg, SUBLANES, CONV_DIM), lambda i: (i, 0, 0))
        state_spec = pl.BlockSpec((n_seg, hp, SSM_STATE), lambda i: (i, 0, 0))
    n_seq = hist.shape[0]
    w_specs = [_resident(w.shape) for w in wts]
    w_specs[1] = _resident((D_MODEL, MAIN_COLS + LANES))
    out_shape = [
        jax.ShapeDtypeStruct((t, D_MODEL), F32),
        jax.ShapeDtypeStruct((n_seq, SUBLANES, CONV_DIM), F32),
        jax.ShapeDtypeStruct(h0.shape, F32),
    ]
    out_specs = [_rows(tm, D_MODEL), hist_spec, state_spec]
    if emit_vf:
        out_shape.append(jax.ShapeDtypeStruct((t, GMLP_WIDTH), F32))
        out_specs.append(_rows(tm, GMLP_WIDTH))
    return pl.pallas_call(
        functools.partial(_mix_kernel, n_seg, carry, tiles_per_seq),
        grid=(n_tiles,),
        in_specs=[_rows(tm, D_MODEL), hist_spec, state_spec] + w_specs,
        out_specs=out_specs,
        out_shape=out_shape,
        scratch_shapes=[pltpu.VMEM((SUBLANES, CONV_DIM), F32),
                        pltpu.VMEM((SSM_GROUPS, SSM_STATE, SSM_GROUP_WIDTH), F32)],
        compiler_params=pltpu.CompilerParams(dimension_semantics=("arbitrary",), vmem_limit_bytes=VMEM_LIMIT),
        name="mix_carry" if carry else "mix_seg",
    )(x, hist, h0, *wts)


def _ffn_kernel(x_ref, pre_w_ref, w_up_ref, w_down_ref, post_w_ref, y_ref):
    x = x_ref[...]
    h = _rms_norm(x, pre_w_ref[...]).astype(BF16)
    acc = None
    for c in range(FFN_HIDDEN // FFN_CHUNK):
        up = _dot(h, w_up_ref[:, c * FFN_CHUNK:(c + 1) * FFN_CHUNK].astype(BF16))
        act = jnp.square(jnp.maximum(up, 0.0)).astype(BF16)
        part = _dot(act, w_down_ref[c * FFN_CHUNK:(c + 1) * FFN_CHUNK, :].astype(BF16))
        acc = part if acc is None else acc + part
    y_ref[...] = x + _rms_norm(acc, post_w_ref[...])


def _ffn_call(x, wts, *, tm):
    t = x.shape[0]
    return pl.pallas_call(
        _ffn_kernel,
        grid=(t // tm,),
        in_specs=[_rows(tm, D_MODEL)] + [_resident(w.shape) for w in wts],
        out_specs=_rows(tm, D_MODEL),
        out_shape=jax.ShapeDtypeStruct((t, D_MODEL), F32),
        compiler_params=pltpu.CompilerParams(dimension_semantics=("arbitrary",), vmem_limit_bytes=VMEM_LIMIT),
        name="ffn",
    )(x, *wts)


def _spatial_tile(ws, bs, n, tm):
    pos = jnp.arange(n)
    mask = (pos[None, :] // CHUNK) <= (pos[:, None] // CHUNK)
    w = jnp.where(mask[None], ws[:, :n, :n], 0.0)
    reps = tm // n
    eye = jnp.eye(reps, dtype=w.dtype)
    tile = jnp.einsum("ab,gij->gaibj", eye, w).reshape(GMLP_GROUPS, tm, tm)
    bias = jnp.repeat(jnp.tile(bs[:, :n].T, (reps, 1)), GMLP_GROUP_DIM, axis=1)
    return tile.astype(BF16), bias.astype(F32)


def _layer(x, hist, h0, *, seq_len, carry, want_v, lw):
    (pre_mix_w, w_in, gmlp_ln_w, gmlp_ln_b, gmlp_ws, gmlp_bs, conv_w, conv_b, dt_bias, a_log, d_skip, ssm_norm_w,
     w_branch_a, w_branch_b, w_out, post_mix_w, pre_ffn_w, w_up, w_down, post_ffn_w) = lw
    n_seq = x.shape[0]
    t = n_seq * seq_len
    tm = TILE_M if carry else SEG_TILE_M
    n_seg = 1 if carry else tm // seq_len
    assert t % tm == 0 and t % FFN_TILE_M == 0 and tm % SSD_BLOCK == 0, (t, tm)
    if carry:
        assert seq_len % tm == 0 and seq_len % GMLP_CHUNK == 0, seq_len
    else:
        assert tm % seq_len == 0 and SUBLANES <= seq_len <= GMLP_CHUNK and seq_len & (seq_len - 1) == 0, seq_len
    xf = x.reshape(t, D_MODEL)
    row = lambda a: a.reshape(1, -1).astype(F32)

    p_dt = MAIN_COLS
    p_ga = p_dt + SSM_HEADS
    wb = w_in.astype(BF16)
    w_tail = jnp.concatenate([w_in[:, p_ga:], w_in[:, p_dt:p_ga],
                              jnp.zeros((D_MODEL, LANES - SSM_HEADS), w_in.dtype)], axis=1).astype(BF16)
    hist8 = jnp.pad(hist.astype(F32), ((0, 0), (SUBLANES - (CONV_WIDTH - 1), 0), (0, 0)))
    h0f = h0.astype(F32).reshape(n_seq, SSM_HEADS * SSM_HEAD_DIM, SSM_STATE)
    wsp, bsp = _spatial_tile(gmlp_ws, gmlp_bs, min(seq_len, GMLP_CHUNK), tm)
    a_pad = jnp.pad(row(-jnp.exp(a_log.astype(F32))), ((0, 0), (0, LANES - SSM_HEADS)))
    dskip_x = jnp.repeat(row(d_skip), SSM_HEAD_DIM, axis=1)
    mix_w = (row(pre_mix_w), wb, w_tail, row(gmlp_ln_w), row(gmlp_ln_b), conv_w.astype(F32), row(conv_b),
             jnp.pad(row(dt_bias), ((0, 0), (0, LANES - SSM_HEADS))),
             wsp, bsp, w_branch_a.astype(BF16), w_branch_b.astype(BF16), w_out.astype(BF16), row(ssm_norm_w),
             row(post_mix_w), a_pad, dskip_x)
    x1, cstate, h_last, *vf = _mix_call(xf, hist8, h0f, mix_w, tm=tm, n_seg=n_seg, carry=carry, seq_len=seq_len,
                                        emit_vf=want_v)

    ffn_w = (row(pre_ffn_w), w_up.astype(F32), w_down.astype(F32), row(post_ffn_w))
    y = _ffn_call(x1, ffn_w, tm=FFN_TILE_M)
    return (y.reshape(n_seq, seq_len, D_MODEL), cstate[:, SUBLANES - (CONV_WIDTH - 1):],
            h_last.reshape(n_seq, SSM_HEADS, SSM_HEAD_DIM, SSM_STATE).astype(h0.dtype),
            vf[0].reshape(n_seq, seq_len, GMLP_WIDTH) if want_v else None)


def kernel(x_prompt, x_sample, cache_conv, state_ssm, pre_mix_w, w_in, gmlp_ln_w, gmlp_ln_b, gmlp_ws, gmlp_bs, conv_w, conv_b, dt_bias, a_log, d_skip, ssm_norm_w, w_branch_a, w_branch_b, w_out, post_mix_w, pre_ffn_w, w_up, w_down, post_ffn_w):
    depth = w_in.shape[0]
    yp, ys = x_prompt, x_sample
    n_p = x_prompt.shape[0]
    hist0 = jnp.zeros((n_p, CONV_WIDTH - 1, CONV_DIM), x_prompt.dtype)
    h00 = jnp.zeros((n_p, SSM_HEADS, SSM_HEAD_DIM, SSM_STATE), state_ssm.dtype)
    conv_p, ssm_p, conv_s, ssm_s, v_s = [], [], [], [], []
    for l in range(depth):
        lw = (pre_mix_w[l], w_in[l], gmlp_ln_w[l], gmlp_ln_b[l], gmlp_ws[l], gmlp_bs[l], conv_w[l], conv_b[l],
              dt_bias[l], a_log[l], d_skip[l], ssm_norm_w[l], w_branch_a[l], w_branch_b[l], w_out[l],
              post_mix_w[l], pre_ffn_w[l], w_up[l], w_down[l], post_ffn_w[l])
        yp, cp, sp, _ = _layer(yp, hist0, h00, seq_len=x_prompt.shape[1], carry=True, want_v=False, lw=lw)
        ys, cs, ss, vs = _layer(ys, cache_conv[l], state_ssm[l], seq_len=x_sample.shape[1], carry=False, want_v=True,
                                lw=lw)
        conv_p.append(cp)
        ssm_p.append(sp)
        conv_s.append(cs)
        ssm_s.append(ss)
        v_s.append(vs)
    return (yp, ys, jnp.stack(conv_p), jnp.stack(ssm_p), jnp.stack(conv_s), jnp.stack(ssm_s), jnp.stack(v_s))
```

```python
import functools

import jax
import jax.numpy as jnp
from jax import lax
from jax.experimental import pallas as pl
from jax.experimental.pallas import tpu as pltpu

D_MODEL = 1024
CHUNK = 64
GMLP_CHUNK = 128
GMLP_WIDTH = 1024
GMLP_GROUPS = 8
GMLP_GROUP_DIM = GMLP_WIDTH // GMLP_GROUPS
SSM_INNER = 2 * D_MODEL
SSM_HEAD_DIM = 64
SSM_HEADS = SSM_INNER // SSM_HEAD_DIM
SSM_GROUPS = 4
SSM_HEADS_PER_GROUP = SSM_HEADS // SSM_GROUPS
SSM_STATE = 128
SSM_GROUP_WIDTH = SSM_INNER // SSM_GROUPS
BC_WIDTH = SSM_GROUPS * SSM_STATE
CONV_WIDTH = 4
CONV_DIM = SSM_INNER + 2 * BC_WIDTH
FFN_HIDDEN = 4 * D_MODEL
EPS = 1e-6
LOG2E = 1.4426950408889634

LANES = 128
SUBLANES = 8
TILE_M = 256
SEG_TILE_M = 128
FFN_TILE_M = 1024
FFN_CHUNK = 1024
SSD_BLOCK = 128
PROJ_CHUNK = 512
VMEM_LIMIT = 60 * 1024 * 1024

BF16 = jnp.bfloat16
F32 = jnp.float32

COL_U = 0
COL_V = COL_U + GMLP_WIDTH
COL_Z = COL_V + GMLP_WIDTH
COL_XBC = COL_Z + SSM_INNER
MAIN_COLS = COL_XBC + CONV_DIM
TAIL_GA = 0
TAIL_GB = TAIL_GA + D_MODEL
TAIL_DT = TAIL_GB + D_MODEL


def _dot(a, b):
    return jnp.dot(a, b, preferred_element_type=F32)


def _dot_nt(a, b):
    return lax.dot_general(a, b, (((1,), (1,)), ((), ())), preferred_element_type=F32)


def _rms_norm(x, w):
    return x * lax.rsqrt(jnp.mean(x * x, axis=-1, keepdims=True) + EPS) * w


def _sigmoid(x):
    return 1.0 / (1.0 + jnp.exp2(x * (-LOG2E)))


def _silu(x):
    hx = 0.5 * x
    return hx + hx * jnp.tanh(hx)


def _gelu_tanh(x):
    c = 0.7978845608028654
    hx = 0.5 * x
    return hx + hx * jnp.tanh(x * (c + (c * 0.044715) * (x * x)))


def _softplus(x):
    return jnp.maximum(x, 0.0) + jnp.log(1.0 + jnp.exp(-jnp.abs(x)))


def _resident(shape):
    nd = len(shape)
    return pl.BlockSpec(shape, lambda i: (0,) * nd, pipeline_mode=pl.Buffered(1))


def _rows(tm, width):
    return pl.BlockSpec((tm, width), lambda i: (i, 0))


def _mix_kernel(n_seg, carry, tiles_per_seq,
                x_ref, hist_ref, h0_ref, pre_w_ref, w_ref, w_tail_ref, ln_w_ref, ln_b_ref, conv_w_ref, conv_b_ref,
                dt_bias_ref, wsp_ref, bsp_ref, w_a_ref, w_b_ref, w_out_ref, nrm_w_ref, post_w_ref, a_ref, dskip_ref,
                y_ref, cstate_ref, hout_ref, *rest):
    tail_ref, state_ref = rest[-2:]
    vf_ref = rest[0] if len(rest) == 3 else None
    tm = x_ref.shape[0]
    seg_len = tm // n_seg
    gw = SSM_GROUP_WIDTH
    cc = PROJ_CHUNK

    if carry:
        first = (pl.program_id(0) % tiles_per_seq) == 0

        @pl.when(first)
        def _():
            for g in range(SSM_GROUPS):
                state_ref[g] = h0_ref[0, g * gw:(g + 1) * gw, :].T

    x = x_ref[...]
    h = _rms_norm(x, pre_w_ref[...]).astype(BF16)

    def proj(col0, c, w=w_ref):
        return _dot(h, w[:, col0 + c * cc:col0 + (c + 1) * cc])

    dt = _softplus(_dot(h, w_tail_ref[:, TAIL_DT:TAIL_DT + LANES]) + dt_bias_ref[...])
    qd = SSD_BLOCK
    qs = min(qd, seg_len)
    blocks_per_seg = seg_len // qs
    seg_shift = seg_len.bit_length() - 1
    rq = lax.broadcasted_iota(jnp.int32, (qd, qd), 0)
    cq = lax.broadcasted_iota(jnp.int32, (qd, qd), 1)
    tri = (rq >= cq) & ((rq >> seg_shift) == (cq >> seg_shift))
    row_in_seg = lax.broadcasted_iota(jnp.int32, (tm, LANES), 0) & (seg_len - 1)
    acum = dt * a_ref[...]
    k = 1
    while k < seg_len:
        acum = acum + jnp.where(row_in_seg >= k, pltpu.roll(acum, k, 0), 0.0)
        k *= 2
    acum2 = acum * LOG2E
    acum2_t = acum2.T
    half = lax.broadcasted_iota(jnp.int32, (tm, LANES), 1) < SSM_HEAD_DIM
    acum_b = [jnp.broadcast_to(acum2[:, hd:hd + 1], (tm, LANES)) for hd in range(SSM_HEADS)]
    dt_b = [jnp.broadcast_to(dt[:, hd:hd + 1], (tm, LANES)) for hd in range(SSM_HEADS)]
    acum_x = jnp.concatenate([jnp.where(half, acum_b[2 * p], acum_b[2 * p + 1]) for p in range(SSM_HEADS // 2)],
                             axis=1)
    dt_x = jnp.concatenate([jnp.where(half, dt_b[2 * p], dt_b[2 * p + 1]) for p in range(SSM_HEADS // 2)], axis=1)
    e_off, to_end = [], []
    for b in range(tm // qs):
        blk = acum_x[b * qs:(b + 1) * qs]
        if b % blocks_per_seg:
            blk = blk - acum_x[b * qs - 1:b * qs]
        e_off.append(jnp.exp2(blk))
        to_end.append(jnp.exp2(blk[qs - 1:qs] - blk))

    def conv_chunk(c):
        cols = slice(c * cc, (c + 1) * cc)
        raw = proj(COL_XBC, c)
        cw = conv_w_ref[:, cols]
        outs = []
        for s in range(n_seg):
            seg = raw[s * seg_len:(s + 1) * seg_len]
            if carry:
                prev = jnp.where(first, hist_ref[0, :, cols], tail_ref[:, cols])
            else:
                prev = hist_ref[s, :, cols]
            xp = jnp.concatenate([prev, seg], axis=0)
            acc = conv_b_ref[:, cols] + seg * cw[CONV_WIDTH - 1:CONV_WIDTH]
            for k in range(1, CONV_WIDTH):
                shifted = pltpu.roll(xp, k, 0)[SUBLANES:]
                acc = acc + shifted * cw[CONV_WIDTH - 1 - k:CONV_WIDTH - k]
            outs.append(_silu(acc))
            cstate_ref[s, :, cols] = seg[seg_len - SUBLANES:]
        if carry:
            tail_ref[:, cols] = raw[tm - SUBLANES:]
        return outs[0] if n_seg == 1 else jnp.concatenate(outs, axis=0)

    xbc = jnp.concatenate([conv_chunk(c) for c in range(CONV_DIM // cc)], axis=1)
    xs = xbc[:, :SSM_INNER]
    bm = xbc[:, SSM_INNER:SSM_INNER + BC_WIDTH].astype(BF16)
    cm = xbc[:, SSM_INNER + BC_WIDTH:].astype(BF16)
    ga = jnp.concatenate([_sigmoid(proj(TAIL_GA, c, w_tail_ref)) for c in range(D_MODEL // cc)], axis=1)
    gb = jnp.concatenate([_sigmoid(proj(TAIL_GB, c, w_tail_ref)) for c in range(D_MODEL // cc)], axis=1)
    zs = jnp.concatenate([_silu(proj(COL_Z, c)) for c in range(SSM_INNER // cc)], axis=1)
    u = jnp.concatenate([_gelu_tanh(proj(COL_U, c)) for c in range(GMLP_WIDTH // cc)], axis=1)
    v = jnp.concatenate([_gelu_tanh(proj(COL_V, c)) for c in range(GMLP_WIDTH // cc)], axis=1)
    mu = jnp.mean(v, axis=-1, keepdims=True)
    vc = v - mu
    var = jnp.mean(vc * vc, axis=-1, keepdims=True)
    vn = vc * lax.rsqrt(var + EPS) * ln_w_ref[...] + ln_b_ref[...]
    if vf_ref is not None:
        vf_ref[...] = vn

    vb = vn.astype(BF16)
    sp, y_a_cols = [], []
    a_cols = D_MODEL // SSM_GROUPS
    per_step = GMLP_GROUPS // (SSM_GROUPS // 2)

    def branch_a_piece(g):
        if g < SSM_GROUPS // 2:
            sp.extend(_dot(wsp_ref[k], vb[:, k * GMLP_GROUP_DIM:(k + 1) * GMLP_GROUP_DIM])
                      for k in range(per_step * g, per_step * (g + 1)))
            return
        if not y_a_cols:
            y_a_cols.append((u * (jnp.concatenate(sp, axis=1) + bsp_ref[...])).astype(BF16))
        k0 = 2 * (g - SSM_GROUPS // 2)
        y_a_cols.extend(_dot(y_a_cols[0], w_a_ref[:, k * a_cols:(k + 1) * a_cols]) for k in (k0, k0 + 1))

    xdt = xs * dt_x
    xdt_b = xdt.astype(BF16)
    xw = [(xdt[b * qs:(b + 1) * qs] * to_end[b]).astype(BF16) for b in range(tm // qs)]
    low_half = lax.broadcasted_iota(jnp.int32, (qd, LANES), 1) < SSM_HEAD_DIM

    y_groups = []
    for g in range(SSM_GROUPS):
        gcols = slice(g * gw, (g + 1) * gw)
        y_blocks = []
        h_run = None
        for d in range(tm // qd):
            d0, d1 = d * qd, (d + 1) * qd
            cg = cm[d0:d1, g * SSM_STATE:(g + 1) * SSM_STATE]
            bg = bm[d0:d1, g * SSM_STATE:(g + 1) * SSM_STATE]
            cb = jnp.where(tri, _dot_nt(cg, bg), 0.0)
            y_pairs = []
            for j in range(SSM_HEADS_PER_GROUP // 2):
                lhs = []
                for hh in (2 * j, 2 * j + 1):
                    hd = g * SSM_HEADS_PER_GROUP + hh
                    seg = acum_b[hd][d0:d1] - acum2_t[hd:hd + 1, d0:d1]
                    lhs.append((jnp.exp2(jnp.minimum(seg, 0.0)) * cb).astype(BF16))
                c0 = g * gw + j * LANES
                pair = xdt_b[d0:d1, c0:c0 + LANES]
                zero = jnp.zeros_like(pair)
                rhs = jnp.concatenate([jnp.where(low_half, pair, zero), jnp.where(low_half, zero, pair)], axis=0)
                y_pairs.append(_dot(jnp.concatenate(lhs, axis=1), rhs))
            y_diag = jnp.concatenate(y_pairs, axis=1)
            y_off = []
            for b in range(d * (qd // qs), (d + 1) * (qd // qs)):
                r0, r1 = b * qs - d0, (b + 1) * qs - d0
                s = b // blocks_per_seg
                if b % blocks_per_seg == 0:
                    h_run = state_ref[g] if carry else h0_ref[s, gcols, :].T
                eo = e_off[b][:, gcols]
                y_off.append(_dot(cg[r0:r1], h_run.astype(BF16)) * eo)
                bt = bg[r0:r1].astype(F32).T.astype(BF16)
                h_run = h_run * eo[qs - 1:qs] + _dot(bt, xw[b][:, gcols])
                if (b + 1) % blocks_per_seg == 0:
                    if carry:
                        state_ref[g] = h_run
                    else:
                        hout_ref[s, gcols, :] = h_run.T
            y_blocks.append(y_diag + (y_off[0] if len(y_off) == 1 else jnp.concatenate(y_off, axis=0)))
        y_groups.append(y_blocks[0] if len(y_blocks) == 1 else jnp.concatenate(y_blocks, axis=0))
        branch_a_piece(g)
    y_a = jnp.concatenate(y_a_cols[1:], axis=1)
    y = jnp.concatenate(y_groups, axis=1) + dskip_ref[...] * xs

    gz = y * zs
    nw = nrm_w_ref[...]
    o_b = []
    for g in range(SSM_GROUPS):
        blk = gz[:, g * gw:(g + 1) * gw]
        blk = blk * lax.rsqrt(jnp.mean(blk * blk, axis=-1, keepdims=True) + EPS)
        o_b.append((blk * nw[:, g * gw:(g + 1) * gw]).astype(BF16))
    y_b = _dot(jnp.concatenate(o_b, axis=1), w_b_ref[...])

    merged = ga * y_a + gb * y_b
    mixed = _dot(merged.astype(BF16), w_out_ref[...])
    y_ref[...] = x + _rms_norm(mixed, post_w_ref[...])

    if carry:
        @pl.when((pl.program_id(0) % tiles_per_seq) == tiles_per_seq - 1)
        def _():
            for g in range(SSM_GROUPS):
                hout_ref[0, g * gw:(g + 1) * gw, :] = state_ref[g].T


def _mix_call(x, hist, h0, wts, *, tm, n_seg, carry, seq_len, emit_vf):
    t = x.shape[0]
    tiles_per_seq = seq_len // tm if carry else 1
    n_tiles = t // tm
    hp = SSM_HEADS * SSM_HEAD_DIM
    if carry:
        hist_spec = pl.BlockSpec((1, SUBLANES, CONV_DIM), lambda i: (i // tiles_per_seq, 0, 0))
        state_spec = pl.BlockSpec((1, hp, SSM_STATE), lambda i: (i // tiles_per_seq, 0, 0))
    else:
        hist_spec = pl.BlockSpec((n_seg, SUBLANES, CONV_DIM), lambda i: (i, 0, 0))
        state_spec = pl.BlockSpec((n_seg, hp, SSM_STATE), lambda i: (i, 0, 0))
    n_seq = hist.shape[0]
    w_specs = [_resident(w.shape) for w in wts]
    w_specs[1] = _resident((D_MODEL, MAIN_COLS + LANES))
    out_shape = [
        jax.ShapeDtypeStruct((t, D_MODEL), F32),
        jax.ShapeDtypeStruct((n_seq, SUBLANES, CONV_DIM), F32),
        jax.ShapeDtypeStruct(h0.shape, F32),
    ]
    out_specs = [_rows(tm, D_MODEL), hist_spec, state_spec]
    if emit_vf:
        out_shape.append(jax.ShapeDtypeStruct((t, GMLP_WIDTH), F32))
        out_specs.append(_rows(tm, GMLP_WIDTH))
    return pl.pallas_call(
        functools.partial(_mix_kernel, n_seg, carry, tiles_per_seq),
        grid=(n_tiles,),
        in_specs=[_rows(tm, D_MODEL), hist_spec, state_spec] + w_specs,
        out_specs=out_specs,
        out_shape=out_shape,
        scratch_shapes=[pltpu.VMEM((SUBLANES, CONV_DIM), F32),
                        pltpu.VMEM((SSM_GROUPS, SSM_STATE, SSM_GROUP_WIDTH), F32)],
        compiler_params=pltpu.CompilerParams(dimension_semantics=("arbitrary",), vmem_limit_bytes=VMEM_LIMIT),
        name="mix_carry" if carry else "mix_seg",
    )(x, hist, h0, *wts)


def _ffn_kernel(x_ref, pre_w_ref, w_up_ref, w_down_ref, post_w_ref, y_ref):
    x = x_ref[...]
    h = _rms_norm(x, pre_w_ref[...]).astype(BF16)
    acc = None
    for c in range(FFN_HIDDEN // FFN_CHUNK):
        up = _dot(h, w_up_ref[:, c * FFN_CHUNK:(c + 1) * FFN_CHUNK].astype(BF16))
        act = jnp.square(jnp.maximum(up, 0.0)).astype(BF16)
        part = _dot(act, w_down_ref[c * FFN_CHUNK:(c + 1) * FFN_CHUNK, :].astype(BF16))
        acc = part if acc is None else acc + part
    y_ref[...] = x + _rms_norm(acc, post_w_ref[...])


def _ffn_call(x, wts, *, tm):
    t = x.shape[0]
    return pl.pallas_call(
        _ffn_kernel,
        grid=(t // tm,),
        in_specs=[_rows(tm, D_MODEL)] + [_resident(w.shape) for w in wts],
        out_specs=_rows(tm, D_MODEL),
        out_shape=jax.ShapeDtypeStruct((t, D_MODEL), F32),
        compiler_params=pltpu.CompilerParams(dimension_semantics=("arbitrary",), vmem_limit_bytes=VMEM_LIMIT),
        name="ffn",
    )(x, *wts)


def _spatial_tile(ws, bs, n, tm):
    pos = jnp.arange(n)
    mask = (pos[None, :] // CHUNK) <= (pos[:, None] // CHUNK)
    w = jnp.where(mask[None], ws[:, :n, :n], 0.0)
    reps = tm // n
    eye = jnp.eye(reps, dtype=w.dtype)
    tile = jnp.einsum("ab,gij->gaibj", eye, w).reshape(GMLP_GROUPS, tm, tm)
    bias = jnp.repeat(jnp.tile(bs[:, :n].T, (reps, 1)), GMLP_GROUP_DIM, axis=1)
    return tile.astype(BF16), bias.astype(F32)


def _layer(x, hist, h0, *, seq_len, carry, want_v, lw):
    (pre_mix_w, w_in, gmlp_ln_w, gmlp_ln_b, gmlp_ws, gmlp_bs, conv_w, conv_b, dt_bias, a_log, d_skip, ssm_norm_w,
     w_branch_a, w_branch_b, w_out, post_mix_w, pre_ffn_w, w_up, w_down, post_ffn_w) = lw
    n_seq = x.shape[0]
    t = n_seq * seq_len
    tm = TILE_M if carry else SEG_TILE_M
    n_seg = 1 if carry else tm // seq_len
    assert t % tm == 0 and t % FFN_TILE_M == 0 and tm % SSD_BLOCK == 0, (t, tm)
    if carry:
        assert seq_len % tm == 0 and seq_len % GMLP_CHUNK == 0, seq_len
    else:
        assert tm % seq_len == 0 and SUBLANES <= seq_len <= GMLP_CHUNK and seq_len & (seq_len - 1) == 0, seq_len
    xf = x.reshape(t, D_MODEL)
    row = lambda a: a.reshape(1, -1).astype(F32)

    p_dt = MAIN_COLS
    p_ga = p_dt + SSM_HEADS
    wb = w_in.astype(BF16)
    w_tail = jnp.concatenate([w_in[:, p_ga:], w_in[:, p_dt:p_ga],
                              jnp.zeros((D_MODEL, LANES - SSM_HEADS), w_in.dtype)], axis=1).astype(BF16)
    hist8 = jnp.pad(hist.astype(F32), ((0, 0), (SUBLANES - (CONV_WIDTH - 1), 0), (0, 0)))
    h0f = h0.astype(F32).reshape(n_seq, SSM_HEADS * SSM_HEAD_DIM, SSM_STATE)
    wsp, bsp = _spatial_tile(gmlp_ws, gmlp_bs, min(seq_len, GMLP_CHUNK), tm)
    a_pad = jnp.pad(row(-jnp.exp(a_log.astype(F32))), ((0, 0), (0, LANES - SSM_HEADS)))
    dskip_x = jnp.repeat(row(d_skip), SSM_HEAD_DIM, axis=1)
    mix_w = (row(pre_mix_w), wb, w_tail, row(gmlp_ln_w), row(gmlp_ln_b), conv_w.astype(F32), row(conv_b),
             jnp.pad(row(dt_bias), ((0, 0), (0, LANES - SSM_HEADS))),
             wsp, bsp, w_branch_a.astype(BF16), w_branch_b.astype(BF16), w_out.astype(BF16), row(ssm_norm_w),
             row(post_mix_w), a_pad, dskip_x)
    x1, cstate, h_last, *vf = _mix_call(xf, hist8, h0f, mix_w, tm=tm, n_seg=n_seg, carry=carry, seq_len=seq_len,
                                        emit_vf=want_v)

    ffn_w = (row(pre_ffn_w), w_up.astype(F32), w_down.astype(F32), row(post_ffn_w))
    y = _ffn_call(x1, ffn_w, tm=FFN_TILE_M)
    return (y.reshape(n_seq, seq_len, D_MODEL), cstate[:, SUBLANES - (CONV_WIDTH - 1):],
            h_last.reshape(n_seq, SSM_HEADS, SSM_HEAD_DIM, SSM_STATE).astype(h0.dtype),
            vf[0].reshape(n_seq, seq_len, GMLP_WIDTH) if want_v else None)


def kernel(x_prompt, x_sample, cache_conv, state_ssm, pre_mix_w, w_in, gmlp_ln_w, gmlp_ln_b, gmlp_ws, gmlp_bs, conv_w, conv_b, dt_bias, a_log, d_skip, ssm_norm_w, w_branch_a, w_branch_b, w_out, post_mix_w, pre_ffn_w, w_up, w_down, post_ffn_w):
    depth = w_in.shape[0]
    yp, ys = x_prompt, x_sample
    n_p = x_prompt.shape[0]
    hist0 = jnp.zeros((n_p, CONV_WIDTH - 1, CONV_DIM), x_prompt.dtype)
    h00 = jnp.zeros((n_p, SSM_HEADS, SSM_HEAD_DIM, SSM_STATE), state_ssm.dtype)
    conv_p, ssm_p, conv_s, ssm_s, v_s = [], [], [], [], []
    for l in range(depth):
        lw = (pre_mix_w[l], w_in[l], gmlp_ln_w[l], gmlp_ln_b[l], gmlp_ws[l], gmlp_bs[l], conv_w[l], conv_b[l],
              dt_bias[l], a_log[l], d_skip[l], ssm_norm_w[l], w_branch_a[l], w_branch_b[l], w_out[l],
              post_mix_w[l], pre_ffn_w[l], w_up[l], w_down[l], post_ffn_w[l])
        yp, cp, sp, _ = _layer(yp, hist0, h00, seq_len=x_prompt.shape[1], carry=True, want_v=False, lw=lw)
        ys, cs, ss, vs = _layer(ys, cache_conv[l], state_ssm[l], seq_len=x_sample.shape[1], carry=False, want_v=True,
                                lw=lw)
        conv_p.append(cp)
        ssm_p.append(sp)
        conv_s.append(cs)
        ssm_s.append(ss)
        v_s.append(vs)
    return (yp, ys, jnp.stack(conv_p), jnp.stack(ssm_p), jnp.stack(conv_s), jnp.stack(ssm_s), jnp.stack(v_s))
```

```python
import functools

import jax
import jax.numpy as jnp
from jax import lax
from jax.experimental import pallas as pl
from jax.experimental.pallas import tpu as pltpu

D_MODEL = 1024
CHUNK = 64
GMLP_CHUNK = 128
GMLP_WIDTH = 1024
GMLP_GROUPS = 8
GMLP_GROUP_DIM = GMLP_WIDTH // GMLP_GROUPS
SSM_INNER = 2 * D_MODEL
SSM_HEAD_DIM = 64
SSM_HEADS = SSM_INNER // SSM_HEAD_DIM
SSM_GROUPS = 4
SSM_HEADS_PER_GROUP = SSM_HEADS // SSM_GROUPS
SSM_STATE = 128
SSM_GROUP_WIDTH = SSM_INNER // SSM_GROUPS
BC_WIDTH = SSM_GROUPS * SSM_STATE
CONV_WIDTH = 4
CONV_DIM = SSM_INNER + 2 * BC_WIDTH
FFN_HIDDEN = 4 * D_MODEL
EPS = 1e-6
LOG2E = 1.4426950408889634

LANES = 128
SUBLANES = 8
TILE_M = 256
SEG_TILE_M = 128
FFN_TILE_M = 1024
FFN_CHUNK = 1024
SSD_BLOCK = 128
PROJ_CHUNK = 512
VMEM_LIMIT = 60 * 1024 * 1024

BF16 = jnp.bfloat16
F32 = jnp.float32

COL_U = 0
COL_V = COL_U + GMLP_WIDTH
COL_Z = COL_V + GMLP_WIDTH
COL_XBC = COL_Z + SSM_INNER
MAIN_COLS = COL_XBC + CONV_DIM
TAIL_GA = 0
TAIL_GB = TAIL_GA + D_MODEL
TAIL_DT = TAIL_GB + D_MODEL


def _dot(a, b):
    return jnp.dot(a, b, preferred_element_type=F32)


def _dot_nt(a, b):
    return lax.dot_general(a, b, (((1,), (1,)), ((), ())), preferred_element_type=F32)


def _rms_norm(x, w):
    return x * lax.rsqrt(jnp.mean(x * x, axis=-1, keepdims=True) + EPS) * w


def _sigmoid(x):
    return 1.0 / (1.0 + jnp.exp2(x * (-LOG2E)))


def _silu(x):
    return x * _sigmoid(x)


def _gelu_tanh(x):
    c = 0.7978845608028654
    return 0.5 * x * (1.0 + jnp.tanh(c * (x + 0.044715 * (x * x * x))))


def _softplus(x):
    return jnp.maximum(x, 0.0) + jnp.log(1.0 + jnp.exp(-jnp.abs(x)))


def _resident(shape):
    nd = len(shape)
    return pl.BlockSpec(shape, lambda i: (0,) * nd, pipeline_mode=pl.Buffered(1))


def _rows(tm, width):
    return pl.BlockSpec((tm, width), lambda i: (i, 0))


def _mix_kernel(n_seg, carry, tiles_per_seq,
                x_ref, hist_ref, h0_ref, pre_w_ref, w_ref, w_tail_ref, ln_w_ref, ln_b_ref, conv_w_ref, conv_b_ref,
                dt_bias_ref, wsp_ref, bsp_ref, w_a_ref, w_b_ref, w_out_ref, nrm_w_ref, post_w_ref, a_ref, dskip_ref,
                y_ref, cstate_ref, hout_ref, *rest):
    tail_ref, state_ref = rest[-2:]
    vf_ref = rest[0] if len(rest) == 3 else None
    tm = x_ref.shape[0]
    seg_len = tm // n_seg
    gw = SSM_GROUP_WIDTH
    cc = PROJ_CHUNK

    if carry:
        first = (pl.program_id(0) % tiles_per_seq) == 0

        @pl.when(first)
        def _():
            for g in range(SSM_GROUPS):
                state_ref[g] = h0_ref[0, g * gw:(g + 1) * gw, :].T

    x = x_ref[...]
    h = _rms_norm(x, pre_w_ref[...]).astype(BF16)

    def proj(col0, c, w=w_ref):
        return _dot(h, w[:, col0 + c * cc:col0 + (c + 1) * cc])

    dt = _softplus(_dot(h, w_tail_ref[:, TAIL_DT:TAIL_DT + LANES]) + dt_bias_ref[...])
    qd = SSD_BLOCK
    qs = min(qd, seg_len)
    blocks_per_seg = seg_len // qs
    seg_shift = seg_len.bit_length() - 1
    rq = lax.broadcasted_iota(jnp.int32, (qd, qd), 0)
    cq = lax.broadcasted_iota(jnp.int32, (qd, qd), 1)
    tri = (rq >= cq) & ((rq >> seg_shift) == (cq >> seg_shift))
    row_in_seg = lax.broadcasted_iota(jnp.int32, (tm, LANES), 0) & (seg_len - 1)
    acum = dt * a_ref[...]
    k = 1
    while k < seg_len:
        acum = acum + jnp.where(row_in_seg >= k, pltpu.roll(acum, k, 0), 0.0)
        k *= 2
    acum2 = acum * LOG2E
    acum2_t = acum2.T
    half = lax.broadcasted_iota(jnp.int32, (tm, LANES), 1) < SSM_HEAD_DIM
    acum_b = [jnp.broadcast_to(acum2[:, hd:hd + 1], (tm, LANES)) for hd in range(SSM_HEADS)]
    dt_b = [jnp.broadcast_to(dt[:, hd:hd + 1], (tm, LANES)) for hd in range(SSM_HEADS)]
    acum_x = jnp.concatenate([jnp.where(half, acum_b[2 * p], acum_b[2 * p + 1]) for p in range(SSM_HEADS // 2)],
                             axis=1)
    dt_x = jnp.concatenate([jnp.where(half, dt_b[2 * p], dt_b[2 * p + 1]) for p in range(SSM_HEADS // 2)], axis=1)
    e_off, to_end = [], []
    for b in range(tm // qs):
        blk = acum_x[b * qs:(b + 1) * qs]
        if b % blocks_per_seg:
            blk = blk - acum_x[b * qs - 1:b * qs]
        e_off.append(jnp.exp2(blk))
        to_end.append(jnp.exp2(blk[qs - 1:qs] - blk))

    def conv_chunk(c):
        cols = slice(c * cc, (c + 1) * cc)
        raw = proj(COL_XBC, c)
        cw = conv_w_ref[:, cols]
        outs = []
        for s in range(n_seg):
            seg = raw[s * seg_len:(s + 1) * seg_len]
            if carry:
                prev = jnp.where(first, hist_ref[0, :, cols], tail_ref[:, cols])
            else:
                prev = hist_ref[s, :, cols]
            xp = jnp.concatenate([prev, seg], axis=0)
            acc = conv_b_ref[:, cols] + seg * cw[CONV_WIDTH - 1:CONV_WIDTH]
            for k in range(1, CONV_WIDTH):
                shifted = pltpu.roll(xp, k, 0)[SUBLANES:]
                acc = acc + shifted * cw[CONV_WIDTH - 1 - k:CONV_WIDTH - k]
            outs.append(_silu(acc))
            cstate_ref[s, :, cols] = seg[seg_len - SUBLANES:]
        if carry:
            tail_ref[:, cols] = raw[tm - SUBLANES:]
        return outs[0] if n_seg == 1 else jnp.concatenate(outs, axis=0)

    xbc = jnp.concatenate([conv_chunk(c) for c in range(CONV_DIM // cc)], axis=1)
    xs = xbc[:, :SSM_INNER]
    bm = xbc[:, SSM_INNER:SSM_INNER + BC_WIDTH].astype(BF16)
    cm = xbc[:, SSM_INNER + BC_WIDTH:].astype(BF16)
    ga = jnp.concatenate([_sigmoid(proj(TAIL_GA, c, w_tail_ref)) for c in range(D_MODEL // cc)], axis=1)
    gb = jnp.concatenate([_sigmoid(proj(TAIL_GB, c, w_tail_ref)) for c in range(D_MODEL // cc)], axis=1)
    zs = jnp.concatenate([_silu(proj(COL_Z, c)) for c in range(SSM_INNER // cc)], axis=1)
    u = jnp.concatenate([_gelu_tanh(proj(COL_U, c)) for c in range(GMLP_WIDTH // cc)], axis=1)
    v = jnp.concatenate([_gelu_tanh(proj(COL_V, c)) for c in range(GMLP_WIDTH // cc)], axis=1)
    mu = jnp.mean(v, axis=-1, keepdims=True)
    vc = v - mu
    var = jnp.mean(vc * vc, axis=-1, keepdims=True)
    vn = vc * lax.rsqrt(var + EPS) * ln_w_ref[...] + ln_b_ref[...]
    if vf_ref is not None:
        vf_ref[...] = vn

    vb = vn.astype(BF16)
    sp, y_a_cols = [], []
    a_cols = D_MODEL // SSM_GROUPS
    per_step = GMLP_GROUPS // (SSM_GROUPS // 2)

    def branch_a_piece(g):
        if g < SSM_GROUPS // 2:
            sp.extend(_dot(wsp_ref[k], vb[:, k * GMLP_GROUP_DIM:(k + 1) * GMLP_GROUP_DIM])
                      for k in range(per_step * g, per_step * (g + 1)))
            return
        if not y_a_cols:
            y_a_cols.append((u * (jnp.concatenate(sp, axis=1) + bsp_ref[...])).astype(BF16))
        k0 = 2 * (g - SSM_GROUPS // 2)
        y_a_cols.extend(_dot(y_a_cols[0], w_a_ref[:, k * a_cols:(k + 1) * a_cols]) for k in (k0, k0 + 1))

    xdt = xs * dt_x
    xdt_b = xdt.astype(BF16)
    xw = [(xdt[b * qs:(b + 1) * qs] * to_end[b]).astype(BF16) for b in range(tm // qs)]
    low_half = lax.broadcasted_iota(jnp.int32, (qd, LANES), 1) < SSM_HEAD_DIM

    y_groups = []
    for g in range(SSM_GROUPS):
        gcols = slice(g * gw, (g + 1) * gw)
        y_blocks = []
        h_run = None
        for d in range(tm // qd):
            d0, d1 = d * qd, (d + 1) * qd
            cg = cm[d0:d1, g * SSM_STATE:(g + 1) * SSM_STATE]
            bg = bm[d0:d1, g * SSM_STATE:(g + 1) * SSM_STATE]
            cb = jnp.where(tri, _dot_nt(cg, bg), 0.0)
            y_pairs = []
            for j in range(SSM_HEADS_PER_GROUP // 2):
                lhs = []
                for hh in (2 * j, 2 * j + 1):
                    hd = g * SSM_HEADS_PER_GROUP + hh
                    seg = acum_b[hd][d0:d1] - acum2_t[hd:hd + 1, d0:d1]
                    lhs.append((jnp.exp2(jnp.minimum(seg, 0.0)) * cb).astype(BF16))
                c0 = g * gw + j * LANES
                pair = xdt_b[d0:d1, c0:c0 + LANES]
                zero = jnp.zeros_like(pair)
                rhs = jnp.concatenate([jnp.where(low_half, pair, zero), jnp.where(low_half, zero, pair)], axis=0)
                y_pairs.append(_dot(jnp.concatenate(lhs, axis=1), rhs))
            y_diag = jnp.concatenate(y_pairs, axis=1)
            y_off = []
            for b in range(d * (qd // qs), (d + 1) * (qd // qs)):
                r0, r1 = b * qs - d0, (b + 1) * qs - d0
                s = b // blocks_per_seg
                if b % blocks_per_seg == 0:
                    h_run = state_ref[g] if carry else h0_ref[s, gcols, :].T
                eo = e_off[b][:, gcols]
                y_off.append(_dot(cg[r0:r1], h_run.astype(BF16)) * eo)
                bt = bg[r0:r1].astype(F32).T.astype(BF16)
                h_run = h_run * eo[qs - 1:qs] + _dot(bt, xw[b][:, gcols])
                if (b + 1) % blocks_per_seg == 0:
                    if carry:
                        state_ref[g] = h_run
                    else:
                        hout_ref[s, gcols, :] = h_run.T
            y_blocks.append(y_diag + (y_off[0] if len(y_off) == 1 else jnp.concatenate(y_off, axis=0)))
        y_groups.append(y_blocks[0] if len(y_blocks) == 1 else jnp.concatenate(y_blocks, axis=0))
        branch_a_piece(g)
    y_a = jnp.concatenate(y_a_cols[1:], axis=1)
    y = jnp.concatenate(y_groups, axis=1) + dskip_ref[...] * xs

    gz = y * zs
    nw = nrm_w_ref[...]
    o_b = []
    for g in range(SSM_GROUPS):
        blk = gz[:, g * gw:(g + 1) * gw]
        blk = blk * lax.rsqrt(jnp.mean(blk * blk, axis=-1, keepdims=True) + EPS)
        o_b.append((blk * nw[:, g * gw:(g + 1) * gw]).astype(BF16))
    y_b = _dot(jnp.concatenate(o_b, axis=1), w_b_ref[...])

    merged = ga * y_a + gb * y_b
    mixed = _dot(merged.astype(BF16), w_out_ref[...])
    y_ref[...] = x + _rms_norm(mixed, post_w_ref[...])

    if carry:
        @pl.when((pl.program_id(0) % tiles_per_seq) == tiles_per_seq - 1)
        def _():
            for g in range(SSM_GROUPS):
                hout_ref[0, g * gw:(g + 1) * gw, :] = state_ref[g].T


def _mix_call(x, hist, h0, wts, *, tm, n_seg, carry, seq_len, emit_vf):
    t = x.shape[0]
    tiles_per_seq = seq_len // tm if carry else 1
    n_tiles = t // tm
    hp = SSM_HEADS * SSM_HEAD_DIM
    if carry:
        hist_spec = pl.BlockSpec((1, SUBLANES, CONV_DIM), lambda i: (i // tiles_per_seq, 0, 0))
        state_spec = pl.BlockSpec((1, hp, SSM_STATE), lambda i: (i // tiles_per_seq, 0, 0))
    else:
        hist_spec = pl.BlockSpec((n_seg, SUBLANES, CONV_DIM), lambda i: (i, 0, 0))
        state_spec = pl.BlockSpec((n_seg, hp, SSM_STATE), lambda i: (i, 0, 0))
    n_seq = hist.shape[0]
    w_specs = [_resident(w.shape) for w in wts]
    w_specs[1] = _resident((D_MODEL, MAIN_COLS + LANES))
    out_shape = [
        jax.ShapeDtypeStruct((t, D_MODEL), F32),
        jax.ShapeDtypeStruct((n_seq, SUBLANES, CONV_DIM), F32),
        jax.ShapeDtypeStruct(h0.shape, F32),
    ]
    out_specs = [_rows(tm, D_MODEL), hist_spec, state_spec]
    if emit_vf:
        out_shape.append(jax.ShapeDtypeStruct((t, GMLP_WIDTH), F32))
        out_specs.append(_rows(tm, GMLP_WIDTH))
    return pl.pallas_call(
        functools.partial(_mix_kernel, n_seg, carry, tiles_per_seq),
        grid=(n_tiles,),
        in_specs=[_rows(tm, D_MODEL), hist_spec, state_spec] + w_specs,
        out_specs=out_specs,
        out_shape=out_shape,
        scratch_shapes=[pltpu.VMEM((SUBLANES, CONV_DIM), F32),
                        pltpu.VMEM((SSM_GROUPS, SSM_STATE, SSM_GROUP_WIDTH), F32)],
        compiler_params=pltpu.CompilerParams(dimension_semantics=("arbitrary",), vmem_limit_bytes=VMEM_LIMIT),
        name="mix_carry" if carry else "mix_seg",
    )(x, hist, h0, *wts)


def _ffn_part(h, w_up, w_down):
    up = _dot(h, w_up.astype(BF16))
    act = jnp.square(jnp.maximum(up, 0.0)).astype(BF16)
    return _dot(act, w_down.astype(BF16))


def _ffn_kernel(x_ref, pre_w_ref, w_up_ref, w_down_ref, post_w_ref, y_ref):
    x = x_ref[...]
    h = _rms_norm(x, pre_w_ref[...]).astype(BF16)
    acc = None
    for c in range(FFN_HIDDEN // FFN_CHUNK):
        cols = slice(c * FFN_CHUNK, (c + 1) * FFN_CHUNK)
        part = _ffn_part(h, w_up_ref[:, cols], w_down_ref[cols, :])
        acc = part if acc is None else acc + part
    y_ref[...] = x + _rms_norm(acc, post_w_ref[...])


def _ffn_stream_kernel(x_ref, pre_w_ref, w_up_ref, w_down_ref, post_w_ref, y_ref, h_ref, acc_ref):
    k = pl.program_id(1)

    @pl.when(k == 0)
    def _():
        h_ref[...] = _rms_norm(x_ref[...], pre_w_ref[...]).astype(BF16)
        acc_ref[...] = jnp.zeros_like(acc_ref)

    acc_ref[...] += _ffn_part(h_ref[...], w_up_ref[...], w_down_ref[...])

    @pl.when(k == pl.num_programs(1) - 1)
    def _():
        y_ref[...] = x_ref[...] + _rms_norm(acc_ref[...], post_w_ref[...])


def _ffn_call(x, wts, *, tm):
    t = x.shape[0]
    out_shape = jax.ShapeDtypeStruct((t, D_MODEL), F32)
    if t > tm:
        return pl.pallas_call(
            _ffn_kernel,
            grid=(t // tm,),
            in_specs=[_rows(tm, D_MODEL)] + [_resident(w.shape) for w in wts],
            out_specs=_rows(tm, D_MODEL),
            out_shape=out_shape,
            compiler_params=pltpu.CompilerParams(dimension_semantics=("arbitrary",), vmem_limit_bytes=VMEM_LIMIT),
            name="ffn",
        )(x, *wts)
    const = lambda shape: pl.BlockSpec(shape, lambda i, k: (0, 0))
    return pl.pallas_call(
        _ffn_stream_kernel,
        grid=(1, FFN_HIDDEN // FFN_CHUNK),
        in_specs=[const((tm, D_MODEL)), const((1, D_MODEL)),
                  pl.BlockSpec((D_MODEL, FFN_CHUNK), lambda i, k: (0, k)),
                  pl.BlockSpec((FFN_CHUNK, D_MODEL), lambda i, k: (k, 0)), const((1, D_MODEL))],
        out_specs=const((tm, D_MODEL)),
        out_shape=out_shape,
        scratch_shapes=[pltpu.VMEM((tm, D_MODEL), BF16), pltpu.VMEM((tm, D_MODEL), F32)],
        compiler_params=pltpu.CompilerParams(dimension_semantics=("arbitrary", "arbitrary"),
                                             vmem_limit_bytes=VMEM_LIMIT),
        name="ffn_stream",
    )(x, *wts)


def _spatial_tile(ws, bs, n, tm):
    pos = jnp.arange(n)
    mask = (pos[None, :] // CHUNK) <= (pos[:, None] // CHUNK)
    w = jnp.where(mask[None], ws[:, :n, :n], 0.0)
    reps = tm // n
    eye = jnp.eye(reps, dtype=w.dtype)
    tile = jnp.einsum("ab,gij->gaibj", eye, w).reshape(GMLP_GROUPS, tm, tm)
    bias = jnp.repeat(jnp.tile(bs[:, :n].T, (reps, 1)), GMLP_GROUP_DIM, axis=1)
    return tile.astype(BF16), bias.astype(F32)


def _layer(x, hist, h0, *, seq_len, carry, want_v, lw):
    (pre_mix_w, w_in, gmlp_ln_w, gmlp_ln_b, gmlp_ws, gmlp_bs, conv_w, conv_b, dt_bias, a_log, d_skip, ssm_norm_w,
     w_branch_a, w_branch_b, w_out, post_mix_w, pre_ffn_w, w_up, w_down, post_ffn_w) = lw
    n_seq = x.shape[0]
    t = n_seq * seq_len
    tm = TILE_M if carry else SEG_TILE_M
    n_seg = 1 if carry else tm // seq_len
    assert t % tm == 0 and t % FFN_TILE_M == 0 and tm % SSD_BLOCK == 0, (t, tm)
    if carry:
        assert seq_len % tm == 0 and seq_len % GMLP_CHUNK == 0, seq_len
    else:
        assert tm % seq_len == 0 and SUBLANES <= seq_len <= GMLP_CHUNK and seq_len & (seq_len - 1) == 0, seq_len
    xf = x.reshape(t, D_MODEL)
    row = lambda a: a.reshape(1, -1).astype(F32)

    p_dt = MAIN_COLS
    p_ga = p_dt + SSM_HEADS
    wb = w_in.astype(BF16)
    w_tail = jnp.concatenate([w_in[:, p_ga:], w_in[:, p_dt:p_ga],
                              jnp.zeros((D_MODEL, LANES - SSM_HEADS), w_in.dtype)], axis=1).astype(BF16)
    hist8 = jnp.pad(hist.astype(F32), ((0, 0), (SUBLANES - (CONV_WIDTH - 1), 0), (0, 0)))
    h0f = h0.astype(F32).reshape(n_seq, SSM_HEADS * SSM_HEAD_DIM, SSM_STATE)
    wsp, bsp = _spatial_tile(gmlp_ws, gmlp_bs, min(seq_len, GMLP_CHUNK), tm)
    a_pad = jnp.pad(row(-jnp.exp(a_log.astype(F32))), ((0, 0), (0, LANES - SSM_HEADS)))
    dskip_x = jnp.repeat(row(d_skip), SSM_HEAD_DIM, axis=1)
    mix_w = (row(pre_mix_w), wb, w_tail, row(gmlp_ln_w), row(gmlp_ln_b), conv_w.astype(F32), row(conv_b),
             jnp.pad(row(dt_bias), ((0, 0), (0, LANES - SSM_HEADS))),
             wsp, bsp, w_branch_a.astype(BF16), w_branch_b.astype(BF16), w_out.astype(BF16), row(ssm_norm_w),
             row(post_mix_w), a_pad, dskip_x)
    x1, cstate, h_last, *vf = _mix_call(xf, hist8, h0f, mix_w, tm=tm, n_seg=n_seg, carry=carry, seq_len=seq_len,
                                        emit_vf=want_v)

    ffn_w = (row(pre_ffn_w), w_up.astype(F32), w_down.astype(F32), row(post_ffn_w))
    y = _ffn_call(x1, ffn_w, tm=FFN_TILE_M)
    return (y.reshape(n_seq, seq_len, D_MODEL), cstate[:, SUBLANES - (CONV_WIDTH - 1):],
            h_last.reshape(n_seq, SSM_HEADS, SSM_HEAD_DIM, SSM_STATE).astype(h0.dtype),
            vf[0].reshape(n_seq, seq_len, GMLP_WIDTH) if want_v else None)


def kernel(x_prompt, x_sample, cache_conv, state_ssm, pre_mix_w, w_in, gmlp_ln_w, gmlp_ln_b, gmlp_ws, gmlp_bs, conv_w, conv_b, dt_bias, a_log, d_skip, ssm_norm_w, w_branch_a, w_branch_b, w_out, post_mix_w, pre_ffn_w, w_up, w_down, post_ffn_w):
    depth = w_in.shape[0]
    yp, ys = x_prompt, x_sample
    n_p = x_prompt.shape[0]
    hist0 = jnp.zeros((n_p, CONV_WIDTH - 1, CONV_DIM), x_prompt.dtype)
    h00 = jnp.zeros((n_p, SSM_HEADS, SSM_HEAD_DIM, SSM_STATE), state_ssm.dtype)
    conv_p, ssm_p, conv_s, ssm_s, v_s = [], [], [], [], []
    for l in range(depth):
        lw = (pre_mix_w[l], w_in[l], gmlp_ln_w[l], gmlp_ln_b[l], gmlp_ws[l], gmlp_bs[l], conv_w[l], conv_b[l],
              dt_bias[l], a_log[l], d_skip[l], ssm_norm_w[l], w_branch_a[l], w_branch_b[l], w_out[l],
              post_mix_w[l], pre_ffn_w[l], w_up[l], w_down[l], post_ffn_w[l])
        yp, cp, sp, _ = _layer(yp, hist0, h00, seq_len=x_prompt.shape[1], carry=True, want_v=False, lw=lw)
        ys, cs, ss, vs = _layer(ys, cache_conv[l], state_ssm[l], seq_len=x_sample.shape[1], carry=False, want_v=True,
                                lw=lw)
        conv_p.append(cp)
        ssm_p.append(sp)
        conv_s.append(cs)
        ssm_s.append(ss)
        v_s.append(vs)
    return (yp, ys, jnp.stack(conv_p), jnp.stack(ssm_p), jnp.stack(conv_s), jnp.stack(ssm_s), jnp.stack(v_s))
```
